```python
import jax, jax.numpy as jnp
from jax import lax
import numpy as np

D_MODEL = 1024
BATCH = 4
SEQ = 4096
DEPTH = 4
DEC_BATCH = 8
DEC_SEQ = 32
PAST_LEN = 4096

CHUNK = 64
NORM_EPS = 1e-6
GLA_HEADS = 4
GLA_DK = D_MODEL // 2
GLA_DV = D_MODEL
GLA_HK = GLA_DK // GLA_HEADS
GLA_HV = GLA_DV // GLA_HEADS
GLA_RANK = 16
GLA_TAU = 16.0
LRU_WIDTH = D_MODEL
LRU_BLOCKS = 8
LRU_BS = LRU_WIDTH // LRU_BLOCKS
CONV_WIDTH = 4
LRU_C = 8.0
RWKV_WIDTH = D_MODEL
RWKV_HD = 64
RWKV_HEADS = RWKV_WIDTH // RWKV_HD
DECAY_RANK = 64
AAA_RANK = 64
RWKV_GN_EPS = 64e-5
SHIFT_WIDTH = 3 * RWKV_WIDTH + DECAY_RANK + AAA_RANK
N_BRANCH = 3
SPLITS = (GLA_DK, GLA_DK, GLA_DV, GLA_RANK, GLA_DV, LRU_WIDTH, LRU_WIDTH, SHIFT_WIDTH, RWKV_WIDTH, N_BRANCH * D_MODEL)
SPLIT_IDX = tuple(int(s) for s in np.cumsum(SPLITS)[:-1])
D_IN = int(sum(SPLITS))
RWKV_SPLIT_IDX = (RWKV_WIDTH, 2 * RWKV_WIDTH, 3 * RWKV_WIDTH, 3 * RWKV_WIDTH + DECAY_RANK)

kernel_name = 'hybrid_gla_rglru_rwkv7_stream_step'


def rmsnorm(x, g):
    xf = x.astype(jnp.float32)
    y = xf * lax.rsqrt(jnp.mean(xf * xf, axis=-1, keepdims=True) + NORM_EPS)
    return (y * g.astype(jnp.float32)).astype(x.dtype)


def gla_branch(q, k, v, gd, w_gup, b_g, norm_g, s0):
    f32 = jnp.float32
    B, L, _ = q.shape
    log_a = jax.nn.log_sigmoid(gd.astype(f32) @ w_gup.astype(f32) + b_g.astype(f32)) / GLA_TAU
    c = CHUNK if L % CHUNK == 0 else L
    n = L // c

    def to_blocks(t, hd):
        return t.astype(f32).reshape(B, n, c, GLA_HEADS, hd).transpose(1, 0, 3, 2, 4)

    qb = to_blocks(q, GLA_HK) * (GLA_HK ** -0.5)
    kb = to_blocks(k, GLA_HK)
    vb = to_blocks(v, GLA_HV)
    ab = to_blocks(log_a, GLA_HK)
    causal = jnp.tril(jnp.ones((c, c), dtype=bool))[:, :, None]

    def step(S, blk):
        qc, kc, vc, ac = blk
        b = jnp.cumsum(ac, axis=2)
        diff = b[:, :, :, None, :] - b[:, :, None, :, :]
        decay = jnp.exp(jnp.where(causal, diff, -jnp.inf))
        scores = jnp.einsum('bhid,bhjd,bhijd->bhij', qc, kc, decay)
        o = (jnp.einsum('bhid,bhde->bhie', qc * jnp.exp(b), S)
             + jnp.einsum('bhij,bhje->bhie', scores, vc))
        b_end = b[:, :, -1:, :]
        S = (S * jnp.exp(b_end)[:, :, 0, :, None]
             + jnp.einsum('bhjd,bhje->bhde', kc * jnp.exp(b_end - b), vc))
        return S, o

    S, o = lax.scan(step, s0.astype(f32), (qb, kb, vb, ab))
    o = o.transpose(1, 0, 3, 2, 4).reshape(B, L, GLA_HEADS, GLA_HV)
    o = o * lax.rsqrt(jnp.mean(o * o, axis=-1, keepdims=True) + NORM_EPS)
    o = o * norm_g.astype(f32).reshape(GLA_HEADS, GLA_HV)
    return o.reshape(B, L, GLA_DV), S


def rglru_branch(xb, conv_w, conv_b, w_a, b_a, w_x, b_x, lam, conv0, h0):
    f32 = jnp.float32
    B, L, W = xb.shape
    xp = jnp.concatenate([conv0.astype(f32), xb.astype(f32)], axis=1)
    cw = conv_w.astype(f32)
    xc = sum(xp[:, t:t + L] * cw[t] for t in range(CONV_WIDTH)) + conv_b.astype(f32)
    new_conv = xp[:, L:]
    xblk = xc.reshape(B, L, LRU_BLOCKS, LRU_BS)
    r = jax.nn.sigmoid(jnp.einsum('blni,nij->blnj', xblk, w_a.astype(f32)).reshape(B, L, W) + b_a.astype(f32))
    i = jax.nn.sigmoid(jnp.einsum('blni,nij->blnj', xblk, w_x.astype(f32)).reshape(B, L, W) + b_x.astype(f32))
    log_a = -LRU_C * r * jax.nn.softplus(-lam.astype(f32))
    a = jnp.exp(log_a)
    u = jnp.sqrt(-jnp.expm1(2.0 * log_a)) * (i * xc)

    def comb(lft, rgt):
        a1, b1 = lft
        a2, b2 = rgt
        return a1 * a2, a2 * b1 + b2

    a_cum, h = lax.associative_scan(comb, (a, u), axis=1)
    h = h + a_cum * h0.astype(f32)[:, None, :]
    return h, new_conv, h[:, -1]


def rwkv7_branch(proj, shift0, mu, w0, w_up, a0, a_up, k_k, k_a, r_k, gn_w, gn_b, s0):
    f32 = jnp.float32
    B, L, _ = proj.shape
    pf = proj.astype(f32)
    prev = jnp.concatenate([shift0.astype(f32)[:, None, :], pf[:, :-1]], axis=1)
    xs = pf + (prev - pf) * mu.astype(f32)
    new_shift = proj[:, -1]
    r, k, v, wd, ad = jnp.split(xs, RWKV_SPLIT_IDX, axis=-1)
    w = w0.astype(f32) + jnp.tanh(wd) @ w_up.astype(f32)
    log_w = -jnp.exp(-jax.nn.softplus(-w) - 0.5)
    a = jax.nn.sigmoid(a0.astype(f32) + ad @ a_up.astype(f32))
    heads = lambda t: t.reshape(B, L, RWKV_HEADS, RWKV_HD)
    kk = heads(k * k_k.astype(f32))
    kk = kk * lax.rsqrt(jnp.sum(kk * kk, axis=-1, keepdims=True) + 1e-12)
    k = k * (1.0 + (a - 1.0) * k_a.astype(f32))
    rh, kh, vh, ah, wh = heads(r), heads(k), heads(v), heads(a), heads(jnp.exp(log_w))
    seq = (jnp.moveaxis(rh, 1, 0), jnp.moveaxis(wh, 1, 0), jnp.moveaxis(kh, 1, 0),
           jnp.moveaxis(vh, 1, 0), jnp.moveaxis(kk, 1, 0), jnp.moveaxis(ah, 1, 0))

    def step(S, inp):
        r_t, w_t, k_t, v_t, kk_t, a_t = inp
        sk = jnp.einsum('bhvk,bhk->bhv', S, kk_t)
        S = (S * w_t[:, :, None, :] - sk[..., None] * (kk_t * a_t)[:, :, None, :]
             + v_t[..., None] * k_t[:, :, None, :])
        return S, jnp.einsum('bhvk,bhk->bhv', S, r_t)

    S, y = lax.scan(step, s0.astype(f32), seq)
    y = jnp.moveaxis(y, 0, 1)
    mean = jnp.mean(y, axis=-1, keepdims=True)
    var = jnp.mean(jnp.square(y - mean), axis=-1, keepdims=True)
    y = ((y - mean) * lax.rsqrt(var + RWKV_GN_EPS) * gn_w.astype(f32).reshape(RWKV_HEADS, RWKV_HD)
         + gn_b.astype(f32).reshape(RWKV_HEADS, RWKV_HD))
    y = y + jnp.sum(rh * kh * r_k.astype(f32), axis=-1, keepdims=True) * vh
    return y.reshape(B, L, RWKV_WIDTH), new_shift, S


def run_trunk(x, gla0, lru_h0, lru_conv0, rwkv0, shift0, P):
    f32 = jnp.float32
    dt = x.dtype
    n_gla, n_h, n_conv, n_rw, n_shift = [], [], [], [], []
    for l in range(DEPTH):
        xn = rmsnorm(x, P['norm_g'][l])
        (q, k, v, gd, z_gla, xl, z_lru, rw, z_rw, gate_logits) = jnp.split(
            xn @ P['w_in'][l], SPLIT_IDX, axis=-1)
        o_gla, s_gla = gla_branch(q, k, v, gd, P['gla_w_gup'][l], P['gla_b_g'][l],
                                  P['gla_norm_g'][l], gla0[l])
        o_lru, conv_new, h_new = rglru_branch(xl, P['lru_conv_w'][l], P['lru_conv_b'][l],
                                              P['lru_w_a'][l], P['lru_b_a'][l], P['lru_w_x'][l],
                                              P['lru_b_x'][l], P['lru_lambda'][l],
                                              lru_conv0[l], lru_h0[l])
        o_rw, shift_new, s_rw = rwkv7_branch(rw, shift0[l], P['rwkv_mu'][l], P['rwkv_w0'][l],
                                             P['rwkv_w_up'][l], P['rwkv_a0'][l], P['rwkv_a_up'][l],
                                             P['rwkv_k_k'][l], P['rwkv_k_a'][l], P['rwkv_r_k'][l],
                                             P['rwkv_gn_w'][l], P['rwkv_gn_b'][l], rwkv0[l])
        g_a, g_b, g_c = jnp.split(jax.nn.sigmoid(gate_logits.astype(f32)), N_BRANCH, axis=-1)
        merged = (g_a * ((o_gla * jax.nn.silu(z_gla.astype(f32))) @ P['w_proj_gla'][l].astype(f32))
                  + g_b * ((o_lru * jax.nn.silu(z_lru.astype(f32))) @ P['w_proj_lru'][l].astype(f32))
                  + g_c * ((o_rw * jax.nn.silu(z_rw.astype(f32))) @ P['w_proj_rwkv'][l].astype(f32)))
        x = x + merged.astype(dt) @ P['w_out'][l]
        n_gla.append(s_gla.astype(dt))
        n_h.append(h_new.astype(dt))
        n_conv.append(conv_new.astype(dt))
        n_rw.append(s_rw.astype(dt))
        n_shift.append(shift_new.astype(dt))
    y = rmsnorm(x, P['final_norm_g'])
    return y, jnp.stack(n_gla), jnp.stack(n_h), jnp.stack(n_conv), jnp.stack(n_rw), jnp.stack(n_shift)


def setup_inputs(seed: int = 0) -> dict:
    key = jax.random.key(seed)
    ks = iter(jax.random.split(key, 48))
    f32 = jnp.float32

    def nrm(shape, s):
        return jax.random.normal(next(ks), shape, f32) * s

    def unif(shape, lo, hi):
        return jax.random.uniform(next(ks), shape, f32, lo, hi)

    x_prompt = nrm((BATCH, SEQ, D_MODEL), 1.0)
    x_sample = nrm((DEC_BATCH, DEC_SEQ, D_MODEL), 1.0)
    state_gla = nrm((DEPTH, DEC_BATCH, GLA_HEADS, GLA_HK, GLA_HV), 0.1)
    state_lru_h = nrm((DEPTH, DEC_BATCH, LRU_WIDTH), 0.5)
    state_lru_conv = nrm((DEPTH, DEC_BATCH, CONV_WIDTH - 1, LRU_WIDTH), 1.0)
    state_rwkv = nrm((DEPTH, DEC_BATCH, RWKV_HEADS, RWKV_HD, RWKV_HD), 0.1)
    state_rwkv_shift = nrm((DEPTH, DEC_BATCH, SHIFT_WIDTH), 1.0)
    norm_g = 1.0 + nrm((DEPTH, D_MODEL), 0.02)
    w_in = nrm((DEPTH, D_MODEL, D_IN), D_MODEL ** -0.5)
    gla_w_gup = nrm((DEPTH, GLA_RANK, GLA_DK), GLA_RANK ** -0.5)
    gla_b_g = 1.0 + nrm((DEPTH, GLA_DK), 0.1)
    gla_norm_g = 1.0 + nrm((DEPTH, GLA_DV), 0.02)
    lru_conv_w = nrm((DEPTH, CONV_WIDTH, LRU_WIDTH), CONV_WIDTH ** -0.5)
    lru_conv_b = nrm((DEPTH, LRU_WIDTH), 0.01)
    lru_w_a = nrm((DEPTH, LRU_BLOCKS, LRU_BS, LRU_BS), LRU_BS ** -0.5)
    lru_b_a = nrm((DEPTH, LRU_WIDTH), 0.01)
    lru_w_x = nrm((DEPTH, LRU_BLOCKS, LRU_BS, LRU_BS), LRU_BS ** -0.5)
    lru_b_x = nrm((DEPTH, LRU_WIDTH), 0.01)
    s = unif((DEPTH, LRU_WIDTH), 0.9, 0.999) ** (1.0 / LRU_C)
    lru_lambda = jnp.log(s) - jnp.log1p(-s)
    rwkv_mu = unif((DEPTH, SHIFT_WIDTH), 0.0, 1.0)
    rwkv_w0 = unif((DEPTH, RWKV_WIDTH), -6.0, -1.0)
    rwkv_w_up = nrm((DEPTH, DECAY_RANK, RWKV_WIDTH), 0.1 * DECAY_RANK ** -0.5)
    rwkv_a0 = nrm((DEPTH, RWKV_WIDTH), 0.1)
    rwkv_a_up = nrm((DEPTH, AAA_RANK, RWKV_WIDTH), AAA_RANK ** -0.5)
    rwkv_k_k = 0.85 + nrm((DEPTH, RWKV_WIDTH), 0.02)
    rwkv_k_a = 1.0 + nrm((DEPTH, RWKV_WIDTH), 0.02)
    rwkv_r_k = nrm((DEPTH, RWKV_HEADS, RWKV_HD), 0.1)
    rwkv_gn_w = 1.0 + nrm((DEPTH, RWKV_WIDTH), 0.02)
    rwkv_gn_b = nrm((DEPTH, RWKV_WIDTH), 0.01)
    w_proj_gla = nrm((DEPTH, GLA_DV, D_MODEL), GLA_DV ** -0.5)
    w_proj_lru = nrm((DEPTH, LRU_WIDTH, D_MODEL), LRU_WIDTH ** -0.5)
    w_proj_rwkv = nrm((DEPTH, RWKV_WIDTH, D_MODEL), RWKV_WIDTH ** -0.5)
    w_out = nrm((DEPTH, D_MODEL, D_MODEL), D_MODEL ** -0.5)
    final_norm_g = 1.0 + nrm((D_MODEL,), 0.02)
    return {
        'x_prompt': x_prompt, 'x_sample': x_sample,
        'state_gla': state_gla, 'state_lru_h': state_lru_h, 'state_lru_conv': state_lru_conv,
        'state_rwkv': state_rwkv, 'state_rwkv_shift': state_rwkv_shift,
        'norm_g': norm_g, 'w_in': w_in,
        'gla_w_gup': gla_w_gup, 'gla_b_g': gla_b_g, 'gla_norm_g': gla_norm_g,
        'lru_conv_w': lru_conv_w, 'lru_conv_b': lru_conv_b, 'lru_w_a': lru_w_a, 'lru_b_a': lru_b_a,
        'lru_w_x': lru_w_x, 'lru_b_x': lru_b_x, 'lru_lambda': lru_lambda,
        'rwkv_mu': rwkv_mu, 'rwkv_w0': rwkv_w0, 'rwkv_w_up': rwkv_w_up, 'rwkv_a0': rwkv_a0,
        'rwkv_a_up': rwkv_a_up, 'rwkv_k_k': rwkv_k_k, 'rwkv_k_a': rwkv_k_a, 'rwkv_r_k': rwkv_r_k,
        'rwkv_gn_w': rwkv_gn_w, 'rwkv_gn_b': rwkv_gn_b,
        'w_proj_gla': w_proj_gla, 'w_proj_lru': w_proj_lru, 'w_proj_rwkv': w_proj_rwkv,
        'w_out': w_out, 'final_norm_g': final_norm_g,
    }


def reference(x_prompt, x_sample, state_gla, state_lru_h, state_lru_conv, state_rwkv, state_rwkv_shift,
              norm_g, w_in, gla_w_gup, gla_b_g, gla_norm_g,
              lru_conv_w, lru_conv_b, lru_w_a, lru_b_a, lru_w_x, lru_b_x, lru_lambda,
              rwkv_mu, rwkv_w0, rwkv_w_up, rwkv_a0, rwkv_a_up, rwkv_k_k, rwkv_k_a, rwkv_r_k,
              rwkv_gn_w, rwkv_gn_b, w_proj_gla, w_proj_lru, w_proj_rwkv, w_out, final_norm_g):
    P = dict(norm_g=norm_g, w_in=w_in, gla_w_gup=gla_w_gup, gla_b_g=gla_b_g, gla_norm_g=gla_norm_g,
             lru_conv_w=lru_conv_w, lru_conv_b=lru_conv_b, lru_w_a=lru_w_a, lru_b_a=lru_b_a,
             lru_w_x=lru_w_x, lru_b_x=lru_b_x, lru_lambda=lru_lambda,
             rwkv_mu=rwkv_mu, rwkv_w0=rwkv_w0, rwkv_w_up=rwkv_w_up, rwkv_a0=rwkv_a0,
             rwkv_a_up=rwkv_a_up, rwkv_k_k=rwkv_k_k, rwkv_k_a=rwkv_k_a, rwkv_r_k=rwkv_r_k,
             rwkv_gn_w=rwkv_gn_w, rwkv_gn_b=rwkv_gn_b, w_proj_gla=w_proj_gla, w_proj_lru=w_proj_lru,
             w_proj_rwkv=w_proj_rwkv, w_out=w_out, final_norm_g=final_norm_g)
    dt = x_prompt.dtype
    B = x_prompt.shape[0]
    gla0 = jnp.zeros((DEPTH, B, GLA_HEADS, GLA_HK, GLA_HV), dt)
    h0 = jnp.zeros((DEPTH, B, LRU_WIDTH), dt)
    conv0 = jnp.zeros((DEPTH, B, CONV_WIDTH - 1, LRU_WIDTH), dt)
    rw0 = jnp.zeros((DEPTH, B, RWKV_HEADS, RWKV_HD, RWKV_HD), dt)
    sh0 = jnp.zeros((DEPTH, B, SHIFT_WIDTH), dt)
    y_prompt, gla_p, lru_h_p, lru_conv_p, rwkv_p, rwkv_shift_p = run_trunk(
        x_prompt, gla0, h0, conv0, rw0, sh0, P)
    y_sample, gla_s, lru_h_s, lru_conv_s, rwkv_s, rwkv_shift_s = run_trunk(
        x_sample, state_gla, state_lru_h, state_lru_conv, state_rwkv, state_rwkv_shift, P)
    return (y_prompt, y_sample,
            gla_p, lru_h_p, lru_conv_p, rwkv_p, rwkv_shift_p,
            gla_s, lru_h_s, lru_conv_s, rwkv_s, rwkv_shift_s)
```

```python
import functools

import jax
import jax.numpy as jnp
from jax import lax
from jax.experimental import pallas as pl
from jax.experimental.pallas import tpu as pltpu

F32 = jnp.float32
BF16 = jnp.bfloat16

D_MODEL = 1024
DEPTH = 4
NORM_EPS = 1e-6
GLA_HEADS = 4
GLA_HK = 128
GLA_HV = 256
GLA_DK = GLA_HEADS * GLA_HK
GLA_DV = GLA_HEADS * GLA_HV
GLA_RANK = 16
GLA_TAU = 16.0
GLA_SUB = 16
LRU_WIDTH = 1024
LRU_BLOCKS = 8
LRU_BS = 128
CONV_WIDTH = 4
LRU_C = 8.0
RWKV_WIDTH = 1024
RWKV_HD = 64
RWKV_HEADS = 16
RWKV_GROUP = 4
RWKV_GW = RWKV_GROUP * RWKV_HD
RWKV_NGROUPS = RWKV_HEADS // RWKV_GROUP
DECAY_RANK = 64
AAA_RANK = 64
RWKV_GN_EPS = 64e-5
SHIFT_WIDTH = 3 * RWKV_WIDTH + DECAY_RANK + AAA_RANK

LANE = 128
SUBLANE = 8

COL_Q = 0
COL_K = 512
COL_V = 1024
COL_ZGLA = 2048
COL_XL = 3072
COL_ZLRU = 4096
COL_R = 5120
COL_KRW = 6144
COL_VRW = 7168
COL_ZRW = 8192
COL_GATES = 9216
COL_GD = 12288
COL_WA = 12416
D_PROJ = 12544
PROJ_TN = 1792

VMEM_LIMIT = 56 * 1024 * 1024


def _cparams(sem):
    return pltpu.CompilerParams(dimension_semantics=sem, vmem_limit_bytes=VMEM_LIMIT)


def _dot(a, b, precision=None):
    return jnp.dot(a, b, preferred_element_type=F32, precision=precision)


def _dot_nt(a, b):
    return lax.dot_general(a, b, (((1,), (1,)), ((), ())), preferred_element_type=F32)


def _dot_tn(a, b):
    return lax.dot_general(a, b, (((0,), (0,)), ((), ())), preferred_element_type=F32)


def _sigmoid(x):
    return 1.0 / (1.0 + jnp.exp(-x))


def _silu(x):
    return x * _sigmoid(x)


def _softplus(x):
    return jnp.maximum(x, 0.0) + jnp.log1p(jnp.exp(-jnp.abs(x)))


def _iota2(shape, dim):
    return lax.broadcasted_iota(jnp.int32, shape, dim)


def _inproj_kernel(x_ref, g_ref, w_ref, o_ref, xn_ref):
    @pl.when(pl.program_id(1) == 0)
    def _():
        x = x_ref[...]
        y = x * lax.rsqrt(jnp.mean(x * x, axis=-1, keepdims=True) + NORM_EPS)
        xn_ref[...] = (y * g_ref[...]).astype(BF16)

    o_ref[...] = _dot(xn_ref[...], w_ref[...])


def _inproj(x2d, g, w):
    T = x2d.shape[0]
    tm = min(T, 512)
    return pl.pallas_call(
        _inproj_kernel,
        grid=(T // tm, D_PROJ // PROJ_TN),
        in_specs=[
            pl.BlockSpec((tm, D_MODEL), lambda i, j: (i, 0)),
            pl.BlockSpec((1, D_MODEL), lambda i, j: (0, 0)),
            pl.BlockSpec((D_MODEL, PROJ_TN), lambda i, j: (0, j)),
        ],
        out_specs=pl.BlockSpec((tm, PROJ_TN), lambda i, j: (i, j)),
        out_shape=jax.ShapeDtypeStruct((T, D_PROJ), F32),
        scratch_shapes=[pltpu.VMEM((tm, D_MODEL), BF16)],
        compiler_params=_cparams(("parallel", "arbitrary")),
        name="inproj",
    )(x2d, g, w)


def _gla_chunk(q, k, v, la, st, C):
    nsub = C // GLA_SUB
    row = _iota2((C, C), 0)
    col = _iota2((C, C), 1)
    tril = (row >= col).astype(F32)
    b = _dot(tril, la, precision=lax.Precision.HIGHEST)
    bend = b[C - 1:C, :]

    qd = q * jnp.exp(b)
    o = _dot_nt(qd.astype(BF16), st.astype(BF16))
    kd = k * jnp.exp(bend - b)
    st_new = st * jnp.exp(bend) + _dot_tn(v.astype(BF16), kd.astype(BF16))

    parts = [jnp.zeros((GLA_SUB, C), F32)]
    for i in range(1, nsub):
        lo = i * GLA_SUB
        beta = b[lo - 1:lo, :]
        qs = q[lo:lo + GLA_SUB, :] * jnp.exp(b[lo:lo + GLA_SUB, :] - beta)
        ks = k * jnp.exp(jnp.minimum(beta - b, 0.0))
        parts.append(_dot_nt(qs.astype(BF16), ks.astype(BF16)))
    a_off = jnp.concatenate(parts, axis=0) if nsub > 1 else parts[0]

    a_diag = jnp.zeros((C, C), F32)
    colmod = col % GLA_SUB
    for jj in range(GLA_SUB):
        bj = jnp.concatenate(
            [jnp.broadcast_to(b[s * GLA_SUB + jj:s * GLA_SUB + jj + 1, :], (GLA_SUB, GLA_HK))
             for s in range(nsub)], axis=0)
        kj = jnp.concatenate(
            [jnp.broadcast_to(k[s * GLA_SUB + jj:s * GLA_SUB + jj + 1, :], (GLA_SUB, GLA_HK))
             for s in range(nsub)], axis=0)
        dec = jnp.exp(jnp.minimum(b - bj, 0.0))
        cj = jnp.sum(q * kj * dec, axis=-1, keepdims=True)
        a_diag = jnp.where(colmod == jj, cj, a_diag)

    rb = row // GLA_SUB
    cb = col // GLA_SUB
    a = jnp.where((rb == cb) & (row >= col), a_diag, jnp.where(rb > cb, a_off, 0.0))
    o = o + _dot(a.astype(BF16), v.astype(BF16))
    return o, st_new


def _gla_kernel(q_ref, k_ref, v_ref, z_ref, gd_ref, wg_ref, bg_ref, ng_ref, s0_ref,
                y_ref, s_ref, st_ref, *, C):
    li = pl.program_id(2)

    @pl.when(li == 0)
    def _():
        st_ref[...] = s0_ref[0, 0]

    TL = q_ref.shape[1]
    for c in range(TL // C):
        sl = slice(c * C, (c + 1) * C)
        q = q_ref[0, sl, :] * (GLA_HK ** -0.5)
        k = k_ref[0, sl, :]
        v = v_ref[0, sl, :]
        x = _dot(gd_ref[0, sl, :].astype(BF16), wg_ref[...]) + bg_ref[...]
        la = (jnp.minimum(x, 0.0) - jnp.log1p(jnp.exp(-jnp.abs(x)))) * (1.0 / GLA_TAU)
        o, st_new = _gla_chunk(q, k, v, la, st_ref[...], C)
        st_ref[...] = st_new
        o = o * lax.rsqrt(jnp.mean(o * o, axis=-1, keepdims=True) + NORM_EPS) * ng_ref[...]
        y_ref[0, sl, :] = (o * _silu(z_ref[0, sl, :])).astype(BF16)

    @pl.when(li == pl.num_programs(2) - 1)
    def _():
        s_ref[0, 0] = st_ref[...]


def _gla(proj, wg_pad, bg, ng, s0t, C):
    B, L, _ = proj.shape
    TL = min(L, 256)
    qb, kb = COL_Q // GLA_HK, COL_K // GLA_HK
    vb, zb = COL_V // GLA_HV, COL_ZGLA // GLA_HV
    gb = COL_GD // LANE
    return pl.pallas_call(
        functools.partial(_gla_kernel, C=C),
        grid=(B, GLA_HEADS, L // TL),
        in_specs=[
            pl.BlockSpec((1, TL, GLA_HK), lambda b, h, l: (b, l, qb + h)),
            pl.BlockSpec((1, TL, GLA_HK), lambda b, h, l: (b, l, kb + h)),
            pl.BlockSpec((1, TL, GLA_HV), lambda b, h, l: (b, l, vb + h)),
            pl.BlockSpec((1, TL, GLA_HV), lambda b, h, l: (b, l, zb + h)),
            pl.BlockSpec((1, TL, LANE), lambda b, h, l: (b, l, gb)),
            pl.BlockSpec((LANE, GLA_HK), lambda b, h, l: (0, h)),
            pl.BlockSpec((1, GLA_HK), lambda b, h, l: (0, h)),
            pl.BlockSpec((1, GLA_HV), lambda b, h, l: (0, h)),
            pl.BlockSpec((1, 1, GLA_HV, GLA_HK), lambda b, h, l: (b, h, 0, 0)),
        ],
        out_specs=[
            pl.BlockSpec((1, TL, GLA_HV), lambda b, h, l: (b, l, h)),
            pl.BlockSpec((1, 1, GLA_HV, GLA_HK), lambda b, h, l: (b, h, 0, 0)),
        ],
        out_shape=[
            jax.ShapeDtypeStruct((B, L, GLA_DV), BF16),
            jax.ShapeDtypeStruct((B, GLA_HEADS, GLA_HV, GLA_HK), F32),
        ],
        scratch_shapes=[pltpu.VMEM((GLA_HV, GLA_HK), F32)],
        compiler_params=_cparams(("parallel", "parallel", "arbitrary")),
        name="gla",
    )(proj, proj, proj, proj, proj, wg_pad, bg, ng, s0t)


def _lru_kernel(x_ref, z_ref, c0_ref, h0_ref, cw_ref, cb_ref, wax_ref, ba_ref, bx_ref, lam_ref,
                y_ref, hl_ref, xpad_ref, a_ref, u_ref, hs_ref, h_ref):
    li = pl.program_id(1)
    TL = x_ref.shape[1]
    PAD = SUBLANE

    @pl.when(li == 0)
    def _():
        xpad_ref[0:PAD, :] = c0_ref[0]
        h_ref[...] = h0_ref[0]

    xpad_ref[PAD:PAD + TL, :] = x_ref[0]
    xc = cb_ref[...]
    for t in range(CONV_WIDTH):
        off = PAD - (CONV_WIDTH - 1) + t
        xc = xc + xpad_ref[off:off + TL, :] * cw_ref[t:t + 1, :]
    xc_b = xc.astype(BF16)
    rs, xs = [], []
    for n in range(LRU_BLOCKS):
        g = _dot(xc_b[:, n * LRU_BS:(n + 1) * LRU_BS], wax_ref[n])
        rs.append(g[:, :LRU_BS])
        xs.append(g[:, LRU_BS:])
    r = _sigmoid(jnp.concatenate(rs, axis=1) + ba_ref[...])
    i = _sigmoid(jnp.concatenate(xs, axis=1) + bx_ref[...])
    log_a = (-LRU_C) * r * _softplus(-lam_ref[...])
    a_ref[...] = jnp.exp(log_a)
    th = jnp.tanh(log_a)
    u_ref[...] = jnp.sqrt(-2.0 * th / (1.0 - th)) * (i * xc)

    def body(g, h):
        base = pl.multiple_of(g * SUBLANE, SUBLANE)
        a8 = a_ref[pl.ds(base, SUBLANE), :]
        u8 = u_ref[pl.ds(base, SUBLANE), :]
        rows = []
        for j in range(SUBLANE):
            h = a8[j:j + 1, :] * h + u8[j:j + 1, :]
            rows.append(h)
        hs_ref[pl.ds(base, SUBLANE), :] = jnp.concatenate(rows, axis=0)
        return h

    h = lax.fori_loop(0, TL // SUBLANE, body, h_ref[...])
    h_ref[...] = h
    hl_ref[0] = h
    y_ref[0] = (hs_ref[...] * _silu(z_ref[0])).astype(BF16)
    xpad_ref[0:PAD, :] = xpad_ref[TL:TL + PAD, :]


def _lru(proj, conv0_pad, h0, cw, cb, wax, ba, bx, lam):
    B, L, _ = proj.shape
    TL = min(L, 256)
    W = LRU_WIDTH
    xb, zb = COL_XL // W, COL_ZLRU // W
    vec = pl.BlockSpec((1, W), lambda b, l: (0, 0))
    return pl.pallas_call(
        _lru_kernel,
        grid=(B, L // TL),
        in_specs=[
            pl.BlockSpec((1, TL, W), lambda b, l: (b, l, xb)),
            pl.BlockSpec((1, TL, W), lambda b, l: (b, l, zb)),
            pl.BlockSpec((1, SUBLANE, W), lambda b, l: (b, 0, 0)),
            pl.BlockSpec((1, 1, W), lambda b, l: (b, 0, 0)),
            pl.BlockSpec((CONV_WIDTH, W), lambda b, l: (0, 0)),
            vec,
            pl.BlockSpec((LRU_BLOCKS, LRU_BS, 2 * LRU_BS), lambda b, l: (0, 0, 0)),
            vec, vec, vec,
        ],
        out_specs=[
            pl.BlockSpec((1, TL, W), lambda b, l: (b, l, 0)),
            pl.BlockSpec((1, 1, W), lambda b, l: (b, 0, 0)),
        ],
        out_shape=[
            jax.ShapeDtypeStruct((B, L, W), BF16),
            jax.ShapeDtypeStruct((B, 1, W), F32),
        ],
        scratch_shapes=[
            pltpu.VMEM((TL + SUBLANE, W), F32),
            pltpu.VMEM((TL, W), F32),
            pltpu.VMEM((TL, W), F32),
            pltpu.VMEM((TL, W), F32),
            pltpu.VMEM((1, W), F32),
        ],
        compiler_params=_cparams(("parallel", "arbitrary")),
        name="rglru",
    )(proj, proj, conv0_pad, h0, cw, cb, wax, ba, bx, lam)


def _split_dot(x, m):
    hi = x.astype(BF16)
    lo = (x - hi.astype(F32)).astype(BF16)
    return _dot(hi, m) + _dot(lo, m)


def _headsum(x, ones_bd):
    outs = [_split_dot(x[:, g * RWKV_GW:(g + 1) * RWKV_GW], ones_bd) for g in range(RWKV_NGROUPS)]
    return jnp.concatenate(outs, axis=1)


def _rwkv_group_chunk(r, k, v, kk, a, lw, s_bd, C, masks):
    tril, strict_rb, incl_rb, eye_rb, bd_sq, bd_cv, bd_state = masks
    lc = _dot(tril, lw, precision=lax.Precision.HIGHEST)
    lend = lc[C - 1:C, :]
    einv = jnp.exp(-lc)
    bt = kk * jnp.exp(lc - lw)
    rt = r * jnp.exp(lc)
    at = -(kk * a) * einv
    kt = k * einv

    def tile_rows(x, n):
        return jnp.concatenate([x] * n, axis=0)

    def blockdiag_cv(x):
        return jnp.where(bd_cv, tile_rows(x, RWKV_GROUP), 0.0).astype(BF16)

    def blockdiag_sq(x):
        return jnp.where(bd_sq, tile_rows(x, RWKV_GROUP), 0.0).astype(BF16)

    br = jnp.concatenate([bt, rt], axis=0).astype(BF16)
    pa = _dot_nt(br, blockdiag_cv(at))
    pk = _dot_nt(br, blockdiag_cv(kt))
    a_ba = jnp.where(strict_rb, pa[:C], 0.0)
    a_ra = jnp.where(incl_rb, pa[C:], 0.0)
    a_bk = jnp.where(strict_rb, pk[:C], 0.0)
    a_rk = jnp.where(incl_rb, pk[C:], 0.0)

    x = a_ba
    xb = blockdiag_sq(x)
    t = eye_rb + a_ba
    n = 2
    while n < C:
        x = _dot(x.astype(BF16), xb)
        xb = blockdiag_sq(x)
        t = t + _dot(t.astype(BF16), xb)
        n *= 2

    g0 = _dot_nt(br, s_bd.astype(BF16))
    v_bd = blockdiag_cv(v)
    rhs = g0[:C] + _dot(a_bk.astype(BF16), v_bd)
    u = _dot(t.astype(BF16), blockdiag_cv(rhs))
    u_bd = blockdiag_cv(u)
    y = g0[C:] + _dot(a_ra.astype(BF16), u_bd) + _dot(a_rk.astype(BF16), v_bd)

    wend = jnp.exp(lend)
    upd = (_dot_tn(u.astype(BF16), (at * wend).astype(BF16))
           + _dot_tn(v.astype(BF16), (kt * wend).astype(BF16)))
    s_new = jnp.where(bd_state, s_bd * wend + upd, 0.0)
    return y, s_new


def _rwkv_masks(C):
    G = RWKV_GROUP
    row = _iota2((C, C), 0)
    col = _iota2((C, C), 1)
    tril = (row >= col).astype(F32)
    r_rb = _iota2((C, G * C), 0)
    c_rb = _iota2((C, G * C), 1) % C
    strict_rb = r_rb > c_rb
    incl_rb = r_rb >= c_rb
    eye_rb = (r_rb == c_rb).astype(F32)
    bd_sq = (_iota2((G * C, G * C), 0) // C) == (_iota2((G * C, G * C), 1) // C)
    bd_cv = (_iota2((G * C, RWKV_GW), 0) // C) == (_iota2((G * C, RWKV_GW), 1) // RWKV_HD)
    bd_state = (_iota2((RWKV_GW, RWKV_GW), 0) // RWKV_HD) == (_iota2((RWKV_GW, RWKV_GW), 1) // RWKV_HD)
    return tril, strict_rb, incl_rb, eye_rb, bd_sq, bd_cv, bd_state


def _rwkv_kernel(r_ref, k_ref, v_ref, wa_ref, z_ref, sh_rkv_ref, sh_wa_ref, s0_ref,
                 mu_rkv_ref, mu_wa_ref, w0_ref, wup_ref, a0_ref, aup_ref, kk_ref, ka_ref, rk_ref,
                 gnw_ref, gnb_ref,
                 y_ref, s_ref,
                 xpad_ref, wapad_ref, st_ref, rs_ref, ks_ref, vs_ref, kks_ref, as_ref, lws_ref, ys_ref,
                 *, C):
    li = pl.program_id(1)
    TL = r_ref.shape[1]
    PAD = SUBLANE
    W = RWKV_WIDTH

    @pl.when(li == 0)
    def _():
        for n in range(3):
            xpad_ref[n, PAD - 1:PAD, :] = sh_rkv_ref[0, :, n * W:(n + 1) * W]
        wapad_ref[PAD - 1:PAD, :] = sh_wa_ref[0]
        for g in range(RWKV_NGROUPS):
            st_ref[g] = s0_ref[0, g]

    mixed = []
    for n, ref in enumerate((r_ref, k_ref, v_ref)):
        x = ref[0]
        xpad_ref[n, PAD:PAD + TL, :] = x
        prev = xpad_ref[n, PAD - 1:PAD - 1 + TL, :]
        mixed.append(x + (prev - x) * mu_rkv_ref[:, n * W:(n + 1) * W])
        xpad_ref[n, PAD - 1:PAD, :] = xpad_ref[n, PAD + TL - 1:PAD + TL, :]
    r, k, v = mixed
    xwa = wa_ref[0]
    wapad_ref[PAD:PAD + TL, :] = xwa
    prev = wapad_ref[PAD - 1:PAD - 1 + TL, :]
    wapad_ref[PAD - 1:PAD, :] = wapad_ref[PAD + TL - 1:PAD + TL, :]
    xwa = xwa + (prev - xwa) * mu_wa_ref[...]

    w = w0_ref[...] + _dot(jnp.tanh(xwa).astype(BF16), wup_ref[...])
    lw = -jnp.exp(-_softplus(-w) - 0.5)
    a = _sigmoid(a0_ref[...] + _dot(xwa.astype(BF16), aup_ref[...]))

    ones_bd = ((_iota2((RWKV_GW, RWKV_GW), 0) // RWKV_HD)
               == (_iota2((RWKV_GW, RWKV_GW), 1) // RWKV_HD)).astype(BF16)
    kk = k * kk_ref[...]
    kk = kk * lax.rsqrt(_headsum(kk * kk, ones_bd) + 1e-12)
    k = k * (1.0 + (a - 1.0) * ka_ref[...])
    rs_ref[...] = r
    ks_ref[...] = k
    vs_ref[...] = v
    kks_ref[...] = kk
    as_ref[...] = a
    lws_ref[...] = lw

    masks = _rwkv_masks(C)

    def chunk_body(c, carry):
        base = pl.multiple_of(c * C, C)
        rows = pl.ds(base, C)
        for g in range(RWKV_NGROUPS):
            cols = slice(g * RWKV_GW, (g + 1) * RWKV_GW)
            y, s_new = _rwkv_group_chunk(
                rs_ref[rows, cols], ks_ref[rows, cols], vs_ref[rows, cols], kks_ref[rows, cols],
                as_ref[rows, cols], lws_ref[rows, cols], st_ref[g], C, masks)
            st_ref[g] = s_new
            ys_ref[rows, cols] = y
        return carry

    lax.fori_loop(0, TL // C, chunk_body, 0)

    y = ys_ref[...]
    r = rs_ref[...]
    k = ks_ref[...]
    v = vs_ref[...]
    inv_hd = 1.0 / RWKV_HD
    mean = _headsum(y, ones_bd) * inv_hd
    d = y - mean
    var = _headsum(d * d, ones_bd) * inv_hd
    yn = d * lax.rsqrt(var + RWKV_GN_EPS) * gnw_ref[...] + gnb_ref[...]
    yn = yn + _headsum(r * k * rk_ref[...], ones_bd) * v
    y_ref[0] = (yn * _silu(z_ref[0])).astype(BF16)

    @pl.when(li == pl.num_programs(1) - 1)
    def _():
        for g in range(RWKV_NGROUPS):
            s_ref[0, g] = st_ref[g]


def _rwkv(proj, sh_rkv, sh_wa, s0_bd, mu_rkv, mu_wa, w0, wup_pad, a0, aup_pad, k_k, k_a, r_k, gn_w, gn_b, C):
    B, L, _ = proj.shape
    TL = min(L, 256)
    W = RWKV_WIDTH
    rb, kb, vb, zb = COL_R // W, COL_KRW // W, COL_VRW // W, COL_ZRW // W
    wab = COL_WA // LANE
    vec = pl.BlockSpec((1, W), lambda b, l: (0, 0))
    mat = pl.BlockSpec((LANE, W), lambda b, l: (0, 0))
    tok = lambda cb: pl.BlockSpec((1, TL, W), lambda b, l: (b, l, cb))
    st_spec = pl.BlockSpec((1, RWKV_NGROUPS, RWKV_GW, RWKV_GW), lambda b, l: (b, 0, 0, 0))
    return pl.pallas_call(
        functools.partial(_rwkv_kernel, C=C),
        grid=(B, L // TL),
        in_specs=[
            tok(rb), tok(kb), tok(vb),
            pl.BlockSpec((1, TL, LANE), lambda b, l: (b, l, wab)),
            tok(zb),
            pl.BlockSpec((1, 1, 3 * W), lambda b, l: (b, 0, 0)),
            pl.BlockSpec((1, 1, LANE), lambda b, l: (b, 0, 0)),
            st_spec,
            pl.BlockSpec((1, 3 * W), lambda b, l: (0, 0)),
            pl.BlockSpec((1, LANE), lambda b, l: (0, 0)),
            vec, mat, vec, mat, vec, vec, vec, vec, vec,
        ],
        out_specs=[
            pl.BlockSpec((1, TL, W), lambda b, l: (b, l, 0)),
            st_spec,
        ],
        out_shape=[
            jax.ShapeDtypeStruct((B, L, W), BF16),
            jax.ShapeDtypeStruct((B, RWKV_NGROUPS, RWKV_GW, RWKV_GW), F32),
        ],
        scratch_shapes=[
            pltpu.VMEM((3, TL + SUBLANE, W), F32),
            pltpu.VMEM((TL + SUBLANE, LANE), F32),
            pltpu.VMEM((RWKV_NGROUPS, RWKV_GW, RWKV_GW), F32),
        ] + [pltpu.VMEM((TL, W), F32)] * 7,
        compiler_params=_cparams(("parallel", "arbitrary")),
        name="rwkv7",
    )(proj, proj, proj, proj, proj, sh_rkv, sh_wa, s0_bd,
      mu_rkv, mu_wa, w0, wup_pad, a0, aup_pad, k_k, k_a, r_k, gn_w, gn_b)


def _outproj_kernel(x_ref, yg_ref, yl_ref, yr_ref, ga_ref, gb_ref, gc_ref,
                    wg_ref, wl_ref, wr_ref, wo_ref, fg_ref, o_ref, *, final):
    merged = (_sigmoid(ga_ref[...]) * _dot(yg_ref[...], wg_ref[...])
              + _sigmoid(gb_ref[...]) * _dot(yl_ref[...], wl_ref[...])
              + _sigmoid(gc_ref[...]) * _dot(yr_ref[...], wr_ref[...]))
    x = x_ref[...] + _dot(merged.astype(BF16), wo_ref[...])
    if final:
        x = x * lax.rsqrt(jnp.mean(x * x, axis=-1, keepdims=True) + NORM_EPS) * fg_ref[...]
    o_ref[...] = x


def _outproj(x2d, yg, yl, yr, proj2d, wg, wl, wr, wo, fg, final):
    T = x2d.shape[0]
    tm = min(T, 256)
    D = D_MODEL
    gb = COL_GATES // D
    tok = lambda cb: pl.BlockSpec((tm, D), lambda i: (i, cb))
    wspec = pl.BlockSpec((D, D), lambda i: (0, 0))
    return pl.pallas_call(
        functools.partial(_outproj_kernel, final=final),
        grid=(T // tm,),
        in_specs=[tok(0), tok(0), tok(0), tok(0), tok(gb), tok(gb + 1), tok(gb + 2),
                  wspec, wspec, wspec, wspec, pl.BlockSpec((1, D), lambda i: (0, 0))],
        out_specs=tok(0),
        out_shape=jax.ShapeDtypeStruct((T, D), F32),
        compiler_params=_cparams(("parallel",)),
        name="outproj",
    )(x2d, yg, yl, yr, proj2d, proj2d, proj2d, wg, wl, wr, wo, fg)


def _prep_layer(P, l):
    w_in = P['w_in'][l]
    o_gd = GLA_DK + GLA_DK + GLA_DV
    o_zg = o_gd + GLA_RANK
    o_xl = o_zg + GLA_DV
    o_zl = o_xl + LRU_WIDTH
    o_rw = o_zl + LRU_WIDTH
    o_wa = o_rw + 3 * RWKV_WIDTH
    o_zr = o_rw + SHIFT_WIDTH
    o_gt = o_zr + RWKV_WIDTH
    w = jnp.concatenate([
        w_in[:, :o_gd],
        w_in[:, o_zg:o_rw],
        w_in[:, o_rw:o_wa],
        w_in[:, o_zr:],
        w_in[:, o_gd:o_zg],
        jnp.zeros((D_MODEL, LANE - GLA_RANK), F32),
        w_in[:, o_wa:o_zr],
    ], axis=1).astype(BF16)
    zpad = jnp.zeros((LANE - GLA_RANK, GLA_DK), F32)
    z64 = jnp.zeros((DECAY_RANK, RWKV_WIDTH), F32)
    mu = P['rwkv_mu'][l]
    row = lambda a: a.reshape(1, -1)
    return dict(
        norm_g=row(P['norm_g'][l]), w_in=w,
        wg_pad=jnp.concatenate([P['gla_w_gup'][l], zpad], axis=0).astype(BF16),
        bg=row(P['gla_b_g'][l]), ng=row(P['gla_norm_g'][l]),
        cw=P['lru_conv_w'][l], cb=row(P['lru_conv_b'][l]),
        wax=jnp.concatenate([P['lru_w_a'][l], P['lru_w_x'][l]], axis=-1).astype(BF16),
        ba=row(P['lru_b_a'][l]), bx=row(P['lru_b_x'][l]), lam=row(P['lru_lambda'][l]),
        mu_rkv=row(mu[:3 * RWKV_WIDTH]), mu_wa=row(mu[3 * RWKV_WIDTH:]),
        w0=row(P['rwkv_w0'][l]),
        wup_pad=jnp.concatenate([P['rwkv_w_up'][l], z64], axis=0).astype(BF16),
        a0=row(P['rwkv_a0'][l]),
        aup_pad=jnp.concatenate([z64, P['rwkv_a_up'][l]], axis=0).astype(BF16),
        k_k=row(P['rwkv_k_k'][l]), k_a=row(P['rwkv_k_a'][l]), r_k=row(P['rwkv_r_k'][l]),
        gn_w=row(P['rwkv_gn_w'][l]), gn_b=row(P['rwkv_gn_b'][l]),
        wpg=P['w_proj_gla'][l].astype(BF16), wpl=P['w_proj_lru'][l].astype(BF16),
        wpr=P['w_proj_rwkv'][l].astype(BF16), wo=P['w_out'][l].astype(BF16),
    )


def _pack_rwkv_state(s):
    B = s.shape[0]
    s = s.reshape(B, RWKV_NGROUPS, RWKV_GROUP, RWKV_HD, RWKV_HD)
    eye = jnp.eye(RWKV_GROUP, dtype=s.dtype)
    out = jnp.einsum('bghvk,hj->bghvjk', s, eye)
    return out.reshape(B, RWKV_NGROUPS, RWKV_GW, RWKV_GW)


def _unpack_rwkv_state(s_bd):
    B = s_bd.shape[0]
    s = s_bd.reshape(B, RWKV_NGROUPS, RWKV_GROUP, RWKV_HD, RWKV_GROUP, RWKV_HD)
    idx = jnp.arange(RWKV_GROUP)
    s = s[:, :, idx, :, idx, :]
    return jnp.moveaxis(s, 0, 2).reshape(B, RWKV_HEADS, RWKV_HD, RWKV_HD)


def _run_trunk(x, gla0, lru_h0, lru_conv0, rwkv0, shift0, layers, final_g, C):
    B, L, D = x.shape
    T = B * L
    x2d = x.reshape(T, D)
    n_gla, n_h, n_conv, n_rw, n_shift = [], [], [], [], []
    for l in range(DEPTH):
        p = layers[l]
        proj2d = _inproj(x2d, p['norm_g'], p['w_in'])
        proj = proj2d.reshape(B, L, D_PROJ)

        yg, s_gla_t = _gla(proj, p['wg_pad'], p['bg'], p['ng'],
                           jnp.swapaxes(gla0[l], -1, -2), C)

        conv0_pad = jnp.concatenate(
            [jnp.zeros((B, SUBLANE - (CONV_WIDTH - 1), LRU_WIDTH), F32), lru_conv0[l]], axis=1)
        yl, h_last = _lru(proj, conv0_pad, lru_h0[l][:, None, :], p['cw'], p['cb'], p['wax'],
                          p['ba'], p['bx'], p['lam'])

        sh = shift0[l]
        yr, s_rw_bd = _rwkv(proj, sh[:, None, :3 * RWKV_WIDTH], sh[:, None, 3 * RWKV_WIDTH:],
                            _pack_rwkv_state(rwkv0[l]),
                            p['mu_rkv'], p['mu_wa'], p['w0'], p['wup_pad'], p['a0'], p['aup_pad'],
                            p['k_k'], p['k_a'], p['r_k'], p['gn_w'], p['gn_b'], C)

        x2d = _outproj(x2d, yg.reshape(T, D), yl.reshape(T, D), yr.reshape(T, D), proj2d,
                       p['wpg'], p['wpl'], p['wpr'], p['wo'], final_g, final=(l == DEPTH - 1))

        n_gla.append(jnp.swapaxes(s_gla_t, -1, -2))
        n_h.append(h_last[:, 0, :])
        n_conv.append(proj[:, L - (CONV_WIDTH - 1):, COL_XL:COL_XL + LRU_WIDTH])
        n_rw.append(_unpack_rwkv_state(s_rw_bd))
        n_shift.append(jnp.concatenate(
            [proj[:, L - 1, COL_R:COL_R + 3 * RWKV_WIDTH], proj[:, L - 1, COL_WA:COL_WA + 2 * DECAY_RANK]],
            axis=-1))
    return (x2d.reshape(B, L, D), jnp.stack(n_gla), jnp.stack(n_h), jnp.stack(n_conv),
            jnp.stack(n_rw), jnp.stack(n_shift))


def kernel(x_prompt, x_sample, state_gla, state_lru_h, state_lru_conv, state_rwkv, state_rwkv_shift,
           norm_g, w_in, gla_w_gup, gla_b_g, gla_norm_g,
           lru_conv_w, lru_conv_b, lru_w_a, lru_b_a, lru_w_x, lru_b_x, lru_lambda,
           rwkv_mu, rwkv_w0, rwkv_w_up, rwkv_a0, rwkv_a_up, rwkv_k_k, rwkv_k_a, rwkv_r_k,
           rwkv_gn_w, rwkv_gn_b, w_proj_gla, w_proj_lru, w_proj_rwkv, w_out, final_norm_g):
    P = dict(norm_g=norm_g, w_in=w_in, gla_w_gup=gla_w_gup, gla_b_g=gla_b_g, gla_norm_g=gla_norm_g,
             lru_conv_w=lru_conv_w, lru_conv_b=lru_conv_b, lru_w_a=lru_w_a, lru_b_a=lru_b_a,
             lru_w_x=lru_w_x, lru_b_x=lru_b_x, lru_lambda=lru_lambda,
             rwkv_mu=rwkv_mu, rwkv_w0=rwkv_w0, rwkv_w_up=rwkv_w_up, rwkv_a0=rwkv_a0,
             rwkv_a_up=rwkv_a_up, rwkv_k_k=rwkv_k_k, rwkv_k_a=rwkv_k_a, rwkv_r_k=rwkv_r_k,
             rwkv_gn_w=rwkv_gn_w, rwkv_gn_b=rwkv_gn_b, w_proj_gla=w_proj_gla, w_proj_lru=w_proj_lru,
             w_proj_rwkv=w_proj_rwkv, w_out=w_out)
    layers = [_prep_layer(P, l) for l in range(DEPTH)]
    final_g = final_norm_g.reshape(1, -1)
    dt = x_prompt.dtype
    B = x_prompt.shape[0]
    zeros = lambda *s: jnp.zeros((DEPTH, B) + s, dt)
    out_p = _run_trunk(x_prompt, zeros(GLA_HEADS, GLA_HK, GLA_HV), zeros(LRU_WIDTH),
                       zeros(CONV_WIDTH - 1, LRU_WIDTH), zeros(RWKV_HEADS, RWKV_HD, RWKV_HD),
                       zeros(SHIFT_WIDTH), layers, final_g, C=64)
    Ls = x_sample.shape[1]
    out_s = _run_trunk(x_sample, state_gla, state_lru_h, state_lru_conv, state_rwkv, state_rwkv_shift,
                       layers, final_g, C=64 if Ls % 64 == 0 else Ls)
    return (out_p[0], out_s[0]) + tuple(out_p[1:]) + tuple(out_s[1:])
```

```python
import functools

import numpy as np
import jax
import jax.numpy as jnp
from jax import lax
from jax.experimental import pallas as pl
from jax.experimental.pallas import tpu as pltpu

F32 = jnp.float32
BF16 = jnp.bfloat16

D_MODEL = 1024
DEPTH = 4
NORM_EPS = 1e-6
GLA_HEADS = 4
GLA_HK = 128
GLA_HV = 256
GLA_DK = GLA_HEADS * GLA_HK
GLA_DV = GLA_HEADS * GLA_HV
GLA_RANK = 16
GLA_TAU = 16.0
LRU_WIDTH = 1024
LRU_BLOCKS = 8
LRU_BS = 128
CONV_WIDTH = 4
LRU_C = 8.0
RWKV_WIDTH = 1024
RWKV_HD = 64
RWKV_HEADS = 16
RWKV_GROUP = 4
RWKV_GW = RWKV_GROUP * RWKV_HD
RWKV_NGROUPS = RWKV_HEADS // RWKV_GROUP
RWKV_LOCAL_CHUNKS = 4
DECAY_RANK = 64
AAA_RANK = 64
RWKV_GN_EPS = 64e-5
RWKV_DECAY_SCALE = 0.6065306597126334
SHIFT_WIDTH = 3 * RWKV_WIDTH + DECAY_RANK + AAA_RANK

LANE = 128
SUBLANE = 8

COL_Q = 0
COL_K = 512
COL_V = 1024
COL_ZGLA = 2048
COL_XL = 3072
COL_ZLRU = 4096
COL_R = 5120
COL_KRW = 6144
COL_VRW = 7168
COL_ZRW = 8192
COL_GATES = 9216
COL_GD = 12288
COL_WA = 12416
D_PROJ = 12544
PROJ_TN = 1792

VMEM_LIMIT = 56 * 1024 * 1024


def _cparams(sem):
    return pltpu.CompilerParams(dimension_semantics=sem, vmem_limit_bytes=VMEM_LIMIT)


def _dot(a, b, precision=None):
    return jnp.dot(a, b, preferred_element_type=F32, precision=precision)


def _dot_nt(a, b):
    return lax.dot_general(a, b, (((1,), (1,)), ((), ())), preferred_element_type=F32)


def _dot_tn(a, b):
    return lax.dot_general(a, b, (((0,), (0,)), ((), ())), preferred_element_type=F32)


def _sigmoid(x):
    return 1.0 / (1.0 + jnp.exp(-x))


def _silu(x):
    return x * _sigmoid(x)


def _softplus(x):
    return jnp.maximum(x, 0.0) + jnp.log1p(jnp.exp(-jnp.abs(x)))


def _iota2(shape, dim):
    return lax.broadcasted_iota(jnp.int32, shape, dim)


def _inproj_kernel(x_ref, g_ref, w_ref, o_ref, xn_ref):
    @pl.when(pl.program_id(1) == 0)
    def _():
        x = x_ref[...]
        y = x * lax.rsqrt(jnp.mean(x * x, axis=-1, keepdims=True) + NORM_EPS)
        xn_ref[...] = (y * g_ref[...]).astype(BF16)

    o_ref[...] = _dot(xn_ref[...], w_ref[...]).astype(o_ref.dtype)


def _inproj(x2d, g, w, out_dtype):
    T = x2d.shape[0]
    tm = min(T, 1024)
    return pl.pallas_call(
        _inproj_kernel,
        grid=(T // tm, D_PROJ // PROJ_TN),
        in_specs=[
            pl.BlockSpec((tm, D_MODEL), lambda i, j: (i, 0)),
            pl.BlockSpec((1, D_MODEL), lambda i, j: (0, 0)),
            pl.BlockSpec((D_MODEL, PROJ_TN), lambda i, j: (0, j)),
        ],
        out_specs=pl.BlockSpec((tm, PROJ_TN), lambda i, j: (i, j)),
        out_shape=jax.ShapeDtypeStruct((T, D_PROJ), out_dtype),
        scratch_shapes=[pltpu.VMEM((tm, D_MODEL), BF16)],
        compiler_params=_cparams(("parallel", "arbitrary")),
        name="inproj",
    )(x2d, g, w)


def _gla_levels(C):
    out, s = [], C // 2
    while s >= 1:
        out.append(s)
        s //= 2
    return out


def _gla_select_matrix(C):
    r = np.arange(C)
    tril = (r[:, None] >= r[None, :])
    blocks = [tril]
    for s in _gla_levels(C):
        boundary = (r // (2 * s)) * (2 * s) + s - 1
        blocks.append(tril[boundary])
    sel = np.concatenate(blocks, axis=0)
    return jnp.asarray(np.concatenate([sel, sel, sel], axis=1), dtype=BF16)


def _gla_kernel(q_ref, k_ref, v_ref, z_ref, gd_ref, wg_ref, bg_ref, ng_ref, s0_ref, gsel_ref,
                y_ref, s_ref, st_ref, *, C):
    li = pl.program_id(2)

    @pl.when(li == 0)
    def _():
        st_ref[...] = s0_ref[0, 0]

    row = _iota2((C, C), 0)
    col = _iota2((C, C), 1)
    eye = row == col
    masks = [((row // (2 * s)) == (col // (2 * s))) & ((row % (2 * s)) >= s) & ((col % (2 * s)) < s)
             for s in _gla_levels(C)]
    gsel = gsel_ref[...]
    TL = q_ref.shape[1]
    slices = [slice(c * C, (c + 1) * C) for c in range(TL // C)]

    def stage_decay(sl):
        x = _dot(gd_ref[0, sl, :], wg_ref[...]) + bg_ref[...]
        la = (jnp.minimum(x, 0.0) - jnp.log1p(jnp.exp(-jnp.abs(x)))) * (1.0 / GLA_TAU)
        return dict(sl=sl, d_all=_cumsum_rows(gsel, la))

    def stage_scores(p):
        sl = p['sl']
        q = q_ref[0, sl, :].astype(F32) * (GLA_HK ** -0.5)
        k = k_ref[0, sl, :].astype(F32)
        d_all = p['d_all']
        b = d_all[:C]
        bend = b[C - 1:C, :]
        a = jnp.where(eye, jnp.sum(q * k, axis=-1, keepdims=True), 0.0)
        for lvl, mask in enumerate(masks):
            d = b - d_all[(lvl + 1) * C:(lvl + 2) * C]
            qs = q * jnp.exp(jnp.minimum(d, 0.0))
            ks = k * jnp.exp(jnp.minimum(-d, 0.0))
            a = jnp.where(mask, _dot_nt(qs.astype(BF16), ks.astype(BF16)), a)
        return dict(sl=sl, a=a.astype(BF16), qd=(q * jnp.exp(b)).astype(BF16),
                    kd=(k * jnp.exp(bend - b)).astype(BF16), wend=jnp.exp(bend))

    def stage_values(p):
        v = v_ref[0, p['sl'], :]
        p.update(o=_dot(p['a'], v), upd=_dot_tn(v, p['kd']))
        return p

    ps = [stage_decay(sl) for sl in slices]
    ps = [stage_scores(p) for p in ps]
    ps = [stage_values(p) for p in ps]

    st = st_ref[...]
    for p in ps:
        sl = p['sl']
        o = p['o'] + _dot_nt(p['qd'], st.astype(BF16))
        st = st * p['wend'] + p['upd']
        o = o * lax.rsqrt(jnp.mean(o * o, axis=-1, keepdims=True) + NORM_EPS) * ng_ref[...]
        y_ref[0, sl, :] = (o * _silu(z_ref[0, sl, :].astype(F32))).astype(BF16)
    st_ref[...] = st

    @pl.when(li == pl.num_programs(2) - 1)
    def _():
        s_ref[0, 0] = st


def _gla(proj, wg_pad, bg, ng, s0t, C):
    B, L, _ = proj.shape
    TL = min(L, 256)
    qb, kb = COL_Q // GLA_HK, COL_K // GLA_HK
    vb, zb = COL_V // GLA_HV, COL_ZGLA // GLA_HV
    gb = COL_GD // LANE
    gsel = _gla_select_matrix(C)
    return pl.pallas_call(
        functools.partial(_gla_kernel, C=C),
        grid=(B, GLA_HEADS, L // TL),
        in_specs=[
            pl.BlockSpec((1, TL, GLA_HK), lambda b, h, l: (b, l, qb + h)),
            pl.BlockSpec((1, TL, GLA_HK), lambda b, h, l: (b, l, kb + h)),
            pl.BlockSpec((1, TL, GLA_HV), lambda b, h, l: (b, l, vb + h)),
            pl.BlockSpec((1, TL, GLA_HV), lambda b, h, l: (b, l, zb + h)),
            pl.BlockSpec((1, TL, LANE), lambda b, h, l: (b, l, gb)),
            pl.BlockSpec((LANE, GLA_HK), lambda b, h, l: (0, h)),
            pl.BlockSpec((1, GLA_HK), lambda b, h, l: (0, h)),
            pl.BlockSpec((1, GLA_HV), lambda b, h, l: (0, h)),
            pl.BlockSpec((1, 1, GLA_HV, GLA_HK), lambda b, h, l: (b, h, 0, 0)),
            pl.BlockSpec(gsel.shape, lambda b, h, l: (0, 0)),
        ],
        out_specs=[
            pl.BlockSpec((1, TL, GLA_HV), lambda b, h, l: (b, l, h)),
            pl.BlockSpec((1, 1, GLA_HV, GLA_HK), lambda b, h, l: (b, h, 0, 0)),
        ],
        out_shape=[
            jax.ShapeDtypeStruct((B, L, GLA_DV), BF16),
            jax.ShapeDtypeStruct((B, GLA_HEADS, GLA_HV, GLA_HK), F32),
        ],
        scratch_shapes=[pltpu.VMEM((GLA_HV, GLA_HK), F32)],
        compiler_params=_cparams(("parallel", "parallel", "arbitrary")),
        name="gla",
    )(proj, proj, proj, proj, proj, wg_pad, bg, ng, s0t, gsel)


def _lru_kernel(x_ref, z_ref, c0_ref, h0_ref, cw_ref, cb_ref, wax_ref, ba_ref, bx_ref, lam_ref,
                y_ref, hl_ref, xpad_ref, a_ref, u_ref, hs_ref, h_ref):
    li = pl.program_id(1)
    TL = x_ref.shape[1]
    PAD = SUBLANE

    @pl.when(li == 0)
    def _():
        xpad_ref[0:PAD, :] = c0_ref[0]
        h_ref[...] = h0_ref[0]

    xpad_ref[PAD:PAD + TL, :] = x_ref[0].astype(F32)
    xc = cb_ref[...]
    for t in range(CONV_WIDTH):
        off = PAD - (CONV_WIDTH - 1) + t
        xc = xc + xpad_ref[off:off + TL, :] * cw_ref[t:t + 1, :]
    xc_b = xc.astype(BF16)
    rs, xs = [], []
    for n in range(LRU_BLOCKS):
        g = _dot(xc_b[:, n * LRU_BS:(n + 1) * LRU_BS], wax_ref[n])
        rs.append(g[:, :LRU_BS])
        xs.append(g[:, LRU_BS:])
    r = _sigmoid(jnp.concatenate(rs, axis=1) + ba_ref[...])
    i = _sigmoid(jnp.concatenate(xs, axis=1) + bx_ref[...])
    log_a = (-LRU_C) * r * _softplus(-lam_ref[...])
    a_ref[...] = jnp.exp(log_a)
    th = jnp.tanh(log_a)
    u_ref[...] = jnp.sqrt(-2.0 * th / (1.0 - th)) * (i * xc)

    row8 = _iota2((SUBLANE, LRU_WIDTH), 0)

    def body(g, h):
        base = pl.multiple_of(g * SUBLANE, SUBLANE)
        a = a_ref[pl.ds(base, SUBLANE), :]
        u = u_ref[pl.ds(base, SUBLANE), :]
        s = 1
        while s < SUBLANE:
            keep = row8 >= s
            u = u + a * jnp.where(keep, pltpu.roll(u, s, axis=0), 0.0)
            a = a * jnp.where(keep, pltpu.roll(a, s, axis=0), 1.0)
            s *= 2
        hs = a * h + u
        hs_ref[pl.ds(base, SUBLANE), :] = hs
        return hs[SUBLANE - 1:SUBLANE, :]

    h = lax.fori_loop(0, TL // SUBLANE, body, h_ref[...], unroll=2)
    h_ref[...] = h
    hl_ref[0] = h
    y_ref[0] = (hs_ref[...] * _silu(z_ref[0].astype(F32))).astype(BF16)
    xpad_ref[0:PAD, :] = xpad_ref[TL:TL + PAD, :]


def _lru(proj, conv0_pad, h0, cw, cb, wax, ba, bx, lam):
    B, L, _ = proj.shape
    TL = min(L, 256)
    W = LRU_WIDTH
    xb, zb = COL_XL // W, COL_ZLRU // W
    vec = pl.BlockSpec((1, W), lambda b, l: (0, 0))
    return pl.pallas_call(
        _lru_kernel,
        grid=(B, L // TL),
        in_specs=[
            pl.BlockSpec((1, TL, W), lambda b, l: (b, l, xb)),
            pl.BlockSpec((1, TL, W), lambda b, l: (b, l, zb)),
            pl.BlockSpec((1, SUBLANE, W), lambda b, l: (b, 0, 0)),
            pl.BlockSpec((1, 1, W), lambda b, l: (b, 0, 0)),
            pl.BlockSpec((CONV_WIDTH, W), lambda b, l: (0, 0)),
            vec,
            pl.BlockSpec((LRU_BLOCKS, LRU_BS, 2 * LRU_BS), lambda b, l: (0, 0, 0)),
            vec, vec, vec,
        ],
        out_specs=[
            pl.BlockSpec((1, TL, W), lambda b, l: (b, l, 0)),
            pl.BlockSpec((1, 1, W), lambda b, l: (b, 0, 0)),
        ],
        out_shape=[
            jax.ShapeDtypeStruct((B, L, W), BF16),
            jax.ShapeDtypeStruct((B, 1, W), F32),
        ],
        scratch_shapes=[
            pltpu.VMEM((TL + SUBLANE, W), F32),
            pltpu.VMEM((TL, W), F32),
            pltpu.VMEM((TL, W), F32),
            pltpu.VMEM((TL, W), F32),
            pltpu.VMEM((1, W), F32),
        ],
        compiler_params=_cparams(("parallel", "arbitrary")),
        name="rglru",
    )(proj, proj, conv0_pad, h0, cw, cb, wax, ba, bx, lam)


def _headsum(x, ones_bd):
    xb = x.astype(BF16)
    outs = [_dot(xb[:, g * RWKV_GW:(g + 1) * RWKV_GW], ones_bd) for g in range(RWKV_NGROUPS)]
    return jnp.concatenate(outs, axis=1)


def _cumsum_rows(sel3, x):
    hi = x.astype(BF16)
    r1 = x - hi.astype(F32)
    mid = r1.astype(BF16)
    lo = (r1 - mid.astype(F32)).astype(BF16)
    return _dot(sel3, jnp.concatenate([hi, mid, lo], axis=0))


def _rwkv_local(loads, C, masks):
    tril_b, strict_rb, incl_rb, eye_rb, bd_sq, bd_cv, bd_state = masks

    def tile_rows(x, n):
        return jnp.concatenate([x] * n, axis=0)

    def blockdiag_cv(x):
        return tile_rows(x.astype(BF16), RWKV_GROUP) * bd_cv

    def blockdiag_sq(x):
        return tile_rows(x.astype(BF16), RWKV_GROUP) * bd_sq

    def stage_decay(load):
        r, k, v, kk, a, lw = load()
        lc = _cumsum_rows(tril_b, lw)
        wend = jnp.exp(lc[C - 1:C, :])
        einv = jnp.exp(-lc)
        bt = kk * jnp.exp(lc - lw)
        rt = r * jnp.exp(lc)
        at = -(kk * a) * einv
        kt = k * einv
        return dict(v=v, wend=wend, bt=bt, rt=rt, at=at, kt=kt)

    def stage_scores(p):
        br = jnp.concatenate([p['bt'], p['rt']], axis=0).astype(BF16)
        pa = _dot_nt(br, blockdiag_cv(p['at']))
        pk = _dot_nt(br, blockdiag_cv(p['kt']))
        a_ba = jnp.where(strict_rb, pa[:C], 0.0)
        p.update(a_ra=jnp.where(incl_rb, pa[C:], 0.0).astype(BF16),
                 a_bk=jnp.where(strict_rb, pk[:C], 0.0).astype(BF16),
                 a_rk=jnp.where(incl_rb, pk[C:], 0.0).astype(BF16),
                 x=a_ba, xb=blockdiag_sq(a_ba), t=eye_rb + a_ba)
        return p

    def stage_double(p):
        x = _dot(p['x'].astype(BF16), p['xb'])
        xb = blockdiag_sq(x)
        p.update(x=x, xb=xb, t=p['t'] + _dot(p['t'].astype(BF16), xb))
        return p

    def stage_apply(p):
        tb = p['t'].astype(BF16)
        v_bd = blockdiag_cv(p['v'])
        p.update(v_bd=v_bd, bhat=_dot(tb, blockdiag_cv(p['bt'])),
                 uloc=_dot(tb, blockdiag_cv(_dot(p['a_bk'], v_bd))))
        return p

    def stage_out(p):
        rhat = p['rt'] + _dot(p['a_ra'], blockdiag_cv(p['bhat']))
        yloc = _dot(p['a_ra'], blockdiag_cv(p['uloc'])) + _dot(p['a_rk'], p['v_bd'])
        atw = (p['at'] * p['wend']).astype(BF16)
        ktw = (p['kt'] * p['wend']).astype(BF16)
        ghat = jnp.where(bd_state, _dot_tn(p['bhat'].astype(BF16), atw), 0.0)
        h = jnp.where(bd_state,
                      _dot_tn(jnp.concatenate([p['uloc'], p['v']], axis=0).astype(BF16),
                              jnp.concatenate([atw, ktw], axis=0)), 0.0)
        return rhat, yloc, ghat, h, p['wend']

    ps = [stage_decay(ld) for ld in loads]
    ps = [stage_scores(p) for p in ps]
    n = 2
    while n < C:
        ps = [stage_double(p) for p in ps]
        n *= 2
    ps = [stage_apply(p) for p in ps]
    return [stage_out(p) for p in ps]


def _rwkv_masks(C):
    G = RWKV_GROUP
    row = _iota2((C, C), 0)
    col = _iota2((C, C), 1)
    tril = (_iota2((C, 3 * C), 0) >= (_iota2((C, 3 * C), 1) % C)).astype(BF16)
    r_rb = _iota2((C, G * C), 0)
    c_rb = _iota2((C, G * C), 1) % C
    strict_rb = r_rb > c_rb
    incl_rb = r_rb >= c_rb
    eye_rb = (r_rb == c_rb).astype(F32)
    bd_sq = ((_iota2((G * C, G * C), 0) // C) == (_iota2((G * C, G * C), 1) // C)).astype(BF16)
    bd_cv = ((_iota2((G * C, RWKV_GW), 0) // C) == (_iota2((G * C, RWKV_GW), 1) // RWKV_HD)).astype(BF16)
    bd_state = (_iota2((RWKV_GW, RWKV_GW), 0) // RWKV_HD) == (_iota2((RWKV_GW, RWKV_GW), 1) // RWKV_HD)
    return tril, strict_rb, incl_rb, eye_rb, bd_sq, bd_cv, bd_state


def _rwkv_kernel(r_ref, k_ref, v_ref, wa_ref, z_ref, sh_rkv_ref, sh_wa_ref, s0_ref,
                 mu_rkv_ref, mu_wa_ref, w0_ref, wup_ref, a0_ref, aup_ref, kk_ref, ka_ref, rk_ref,
                 gnw_ref, gnb_ref,
                 y_ref, s_ref,
                 xpad_ref, wapad_ref, st_ref, rs_ref, ks_ref, vs_ref, kks_ref, as_ref, lws_ref, ys_ref,
                 rhat_ref, ghat_ref, h_ref, wend_ref, *, C):
    li = pl.program_id(1)
    TL = r_ref.shape[1]
    PAD = SUBLANE
    W = RWKV_WIDTH

    @pl.when(li == 0)
    def _():
        for n in range(3):
            xpad_ref[n, PAD - 1:PAD, :] = sh_rkv_ref[0, :, n * W:(n + 1) * W]
        wapad_ref[PAD - 1:PAD, :] = sh_wa_ref[0]
        for g in range(RWKV_NGROUPS):
            st_ref[g] = s0_ref[0, g]

    mixed = []
    for n, ref in enumerate((r_ref, k_ref, v_ref)):
        x = ref[0].astype(F32)
        xpad_ref[n, PAD:PAD + TL, :] = x
        prev = xpad_ref[n, PAD - 1:PAD - 1 + TL, :]
        mixed.append(x + (prev - x) * mu_rkv_ref[:, n * W:(n + 1) * W])
        xpad_ref[n, PAD - 1:PAD, :] = xpad_ref[n, PAD + TL - 1:PAD + TL, :]
    r, k, v = mixed
    xwa = wa_ref[0].astype(F32)
    wapad_ref[PAD:PAD + TL, :] = xwa
    prev = wapad_ref[PAD - 1:PAD - 1 + TL, :]
    wapad_ref[PAD - 1:PAD, :] = wapad_ref[PAD + TL - 1:PAD + TL, :]
    xwa = xwa + (prev - xwa) * mu_wa_ref[...]

    w = w0_ref[...] + _dot(jnp.tanh(xwa).astype(BF16), wup_ref[...])
    lw = (-RWKV_DECAY_SCALE) * _sigmoid(w)
    a = _sigmoid(a0_ref[...] + _dot(xwa.astype(BF16), aup_ref[...]))

    ones_bd = ((_iota2((RWKV_GW, RWKV_GW), 0) // RWKV_HD)
               == (_iota2((RWKV_GW, RWKV_GW), 1) // RWKV_HD)).astype(BF16)
    kk = k * kk_ref[...]
    kk = kk * lax.rsqrt(_headsum(kk * kk, ones_bd) + 1e-12)
    k = k * (1.0 + (a - 1.0) * ka_ref[...])
    rs_ref[...] = r
    ks_ref[...] = k
    vs_ref[...] = v
    kks_ref[...] = kk
    as_ref[...] = a
    lws_ref[...] = lw

    masks = _rwkv_masks(C)

    NCH = TL // C
    NB = min(NCH, RWKV_LOCAL_CHUNKS)

    def local_body(cb, carry):
        probs = []
        for j in range(NB):
            c = cb * NB + j
            rows = pl.ds(pl.multiple_of(c * C, C), C)
            for g in range(RWKV_NGROUPS):
                probs.append((c, rows, g, slice(g * RWKV_GW, (g + 1) * RWKV_GW)))

        def loader(rows, cols):
            return lambda: tuple(ref[rows, cols]
                                 for ref in (rs_ref, ks_ref, vs_ref, kks_ref, as_ref, lws_ref))

        outs = _rwkv_local([loader(rows, cols) for _, rows, _, cols in probs], C, masks)
        for (c, rows, g, cols), (rhat, yloc, ghat, h, wend) in zip(probs, outs):
            rhat_ref[rows, cols] = rhat.astype(BF16)
            ys_ref[rows, cols] = yloc
            ghat_ref[c, g] = ghat.astype(BF16)
            h_ref[c, g] = h
            wend_ref[c, :, cols] = wend
        return carry

    lax.fori_loop(0, NCH // NB, local_body, 0)

    def state_body(c, carry):
        base = pl.multiple_of(c * C, C)
        rows = pl.ds(base, C)
        for g in range(RWKV_NGROUPS):
            cols = slice(g * RWKV_GW, (g + 1) * RWKV_GW)
            s = st_ref[g]
            sb = s.astype(BF16)
            ys_ref[rows, cols] = ys_ref[rows, cols] + _dot_nt(rhat_ref[rows, cols], sb)
            st_ref[g] = s * wend_ref[c, :, cols] + _dot(sb, ghat_ref[c, g]) + h_ref[c, g]
        return carry

    lax.fori_loop(0, TL // C, state_body, 0)

    y = ys_ref[...]
    r = rs_ref[...]
    k = ks_ref[...]
    v = vs_ref[...]
    inv_hd = 1.0 / RWKV_HD
    mean = _headsum(y, ones_bd) * inv_hd
    d = y - mean
    var = _headsum(d * d, ones_bd) * inv_hd
    yn = d * lax.rsqrt(var + RWKV_GN_EPS) * gnw_ref[...] + gnb_ref[...]
    yn = yn + _headsum(r * k * rk_ref[...], ones_bd) * v
    y_ref[0] = (yn * _silu(z_ref[0].astype(F32))).astype(BF16)

    @pl.when(li == pl.num_programs(1) - 1)
    def _():
        for g in range(RWKV_NGROUPS):
            s_ref[0, g] = st_ref[g]


def _rwkv(proj, sh_rkv, sh_wa, s0_bd, mu_rkv, mu_wa, w0, wup_pad, a0, aup_pad, k_k, k_a, r_k, gn_w, gn_b, C):
    B, L, _ = proj.shape
    TL = min(L, 256)
    W = RWKV_WIDTH
    rb, kb, vb, zb = COL_R // W, COL_KRW // W, COL_VRW // W, COL_ZRW // W
    wab = COL_WA // LANE
    vec = pl.BlockSpec((1, W), lambda b, l: (0, 0))
    mat = pl.BlockSpec((LANE, W), lambda b, l: (0, 0))
    tok = lambda cb: pl.BlockSpec((1, TL, W), lambda b, l: (b, l, cb))
    st_spec = pl.BlockSpec((1, RWKV_NGROUPS, RWKV_GW, RWKV_GW), lambda b, l: (b, 0, 0, 0))
    return pl.pallas_call(
        functools.partial(_rwkv_kernel, C=C),
        grid=(B, L // TL),
        in_specs=[
            tok(rb), tok(kb), tok(vb),
            pl.BlockSpec((1, TL, LANE), lambda b, l: (b, l, wab)),
            tok(zb),
            pl.BlockSpec((1, 1, 3 * W), lambda b, l: (b, 0, 0)),
            pl.BlockSpec((1, 1, LANE), lambda b, l: (b, 0, 0)),
            st_spec,
            pl.BlockSpec((1, 3 * W), lambda b, l: (0, 0)),
            pl.BlockSpec((1, LANE), lambda b, l: (0, 0)),
            vec, mat, vec, mat, vec, vec, vec, vec, vec,
        ],
        out_specs=[
            pl.BlockSpec((1, TL, W), lambda b, l: (b, l, 0)),
            st_spec,
        ],
        out_shape=[
            jax.ShapeDtypeStruct((B, L, W), BF16),
            jax.ShapeDtypeStruct((B, RWKV_NGROUPS, RWKV_GW, RWKV_GW), F32),
        ],
        scratch_shapes=[
            pltpu.VMEM((3, TL + SUBLANE, W), F32),
            pltpu.VMEM((TL + SUBLANE, LANE), F32),
            pltpu.VMEM((RWKV_NGROUPS, RWKV_GW, RWKV_GW), F32),
        ] + [pltpu.VMEM((TL, W), F32)] * 7 + [
            pltpu.VMEM((TL, W), BF16),
            pltpu.VMEM((TL // C, RWKV_NGROUPS, RWKV_GW, RWKV_GW), BF16),
            pltpu.VMEM((TL // C, RWKV_NGROUPS, RWKV_GW, RWKV_GW), F32),
            pltpu.VMEM((TL // C, 1, W), F32),
        ],
        compiler_params=_cparams(("parallel", "arbitrary")),
        name="rwkv7",
    )(proj, proj, proj, proj, proj, sh_rkv, sh_wa, s0_bd,
      mu_rkv, mu_wa, w0, wup_pad, a0, aup_pad, k_k, k_a, r_k, gn_w, gn_b)


def _outproj_kernel(x_ref, yg_ref, yl_ref, yr_ref, ga_ref, gb_ref, gc_ref,
                    wg_ref, wl_ref, wr_ref, wo_ref, fg_ref, o_ref, *, final):
    merged = (_sigmoid(ga_ref[...].astype(F32)) * _dot(yg_ref[...], wg_ref[...])
              + _sigmoid(gb_ref[...].astype(F32)) * _dot(yl_ref[...], wl_ref[...])
              + _sigmoid(gc_ref[...].astype(F32)) * _dot(yr_ref[...], wr_ref[...]))
    x = x_ref[...] + _dot(merged.astype(BF16), wo_ref[...])
    if final:
        x = x * lax.rsqrt(jnp.mean(x * x, axis=-1, keepdims=True) + NORM_EPS) * fg_ref[...]
    o_ref[...] = x


def _outproj(x2d, yg, yl, yr, proj2d, wg, wl, wr, wo, fg, final):
    T = x2d.shape[0]
    tm = min(T, 256)
    D = D_MODEL
    gb = COL_GATES // D
    tok = lambda cb: pl.BlockSpec((tm, D), lambda i: (i, cb))
    wspec = pl.BlockSpec((D, D), lambda i: (0, 0))
    return pl.pallas_call(
        functools.partial(_outproj_kernel, final=final),
        grid=(T // tm,),
        in_specs=[tok(0), tok(0), tok(0), tok(0), tok(gb), tok(gb + 1), tok(gb + 2),
                  wspec, wspec, wspec, wspec, pl.BlockSpec((1, D), lambda i: (0, 0))],
        out_specs=tok(0),
        out_shape=jax.ShapeDtypeStruct((T, D), F32),
        compiler_params=_cparams(("parallel",)),
        name="outproj",
    )(x2d, yg, yl, yr, proj2d, proj2d, proj2d, wg, wl, wr, wo, fg)


def _prep_layer(P, l):
    w_in = P['w_in'][l]
    o_gd = GLA_DK + GLA_DK + GLA_DV
    o_zg = o_gd + GLA_RANK
    o_xl = o_zg + GLA_DV
    o_zl = o_xl + LRU_WIDTH
    o_rw = o_zl + LRU_WIDTH
    o_wa = o_rw + 3 * RWKV_WIDTH
    o_zr = o_rw + SHIFT_WIDTH
    o_gt = o_zr + RWKV_WIDTH
    w = jnp.concatenate([
        w_in[:, :o_gd],
        w_in[:, o_zg:o_rw],
        w_in[:, o_rw:o_wa],
        w_in[:, o_zr:],
        w_in[:, o_gd:o_zg],
        jnp.zeros((D_MODEL, LANE - GLA_RANK), F32),
        w_in[:, o_wa:o_zr],
    ], axis=1).astype(BF16)
    zpad = jnp.zeros((LANE - GLA_RANK, GLA_DK), F32)
    z64 = jnp.zeros((DECAY_RANK, RWKV_WIDTH), F32)
    mu = P['rwkv_mu'][l]
    row = lambda a: a.reshape(1, -1)
    return dict(
        norm_g=row(P['norm_g'][l]), w_in=w,
        wg_pad=jnp.concatenate([P['gla_w_gup'][l], zpad], axis=0).astype(BF16),
        bg=row(P['gla_b_g'][l]), ng=row(P['gla_norm_g'][l]),
        cw=P['lru_conv_w'][l], cb=row(P['lru_conv_b'][l]),
        wax=jnp.concatenate([P['lru_w_a'][l], P['lru_w_x'][l]], axis=-1).astype(BF16),
        ba=row(P['lru_b_a'][l]), bx=row(P['lru_b_x'][l]), lam=row(P['lru_lambda'][l]),
        mu_rkv=row(mu[:3 * RWKV_WIDTH]), mu_wa=row(mu[3 * RWKV_WIDTH:]),
        w0=row(P['rwkv_w0'][l]),
        wup_pad=jnp.concatenate([P['rwkv_w_up'][l], z64], axis=0).astype(BF16),
        a0=row(P['rwkv_a0'][l]),
        aup_pad=jnp.concatenate([z64, P['rwkv_a_up'][l]], axis=0).astype(BF16),
        k_k=row(P['rwkv_k_k'][l]), k_a=row(P['rwkv_k_a'][l]), r_k=row(P['rwkv_r_k'][l]),
        gn_w=row(P['rwkv_gn_w'][l]), gn_b=row(P['rwkv_gn_b'][l]),
        wpg=P['w_proj_gla'][l].astype(BF16), wpl=P['w_proj_lru'][l].astype(BF16),
        wpr=P['w_proj_rwkv'][l].astype(BF16), wo=P['w_out'][l].astype(BF16),
    )


def _pack_rwkv_state(s):
    B = s.shape[0]
    s = s.reshape(B, RWKV_NGROUPS, RWKV_GROUP, RWKV_HD, RWKV_HD)
    eye = jnp.eye(RWKV_GROUP, dtype=s.dtype)
    out = jnp.einsum('bghvk,hj->bghvjk', s, eye)
    return out.reshape(B, RWKV_NGROUPS, RWKV_GW, RWKV_GW)


def _unpack_rwkv_state(s_bd):
    B = s_bd.shape[0]
    s = s_bd.reshape(B, RWKV_NGROUPS, RWKV_GROUP, RWKV_HD, RWKV_GROUP, RWKV_HD)
    idx = jnp.arange(RWKV_GROUP)
    s = s[:, :, idx, :, idx, :]
    return jnp.moveaxis(s, 0, 2).reshape(B, RWKV_HEADS, RWKV_HD, RWKV_HD)


def _run_trunk(x, gla0, lru_h0, lru_conv0, rwkv0, shift0, layers, final_g, C):
    B, L, D = x.shape
    T = B * L
    x2d = x.reshape(T, D)
    n_gla, n_h, n_conv, n_rw, n_shift = [], [], [], [], []
    for l in range(DEPTH):
        p = layers[l]
        proj2d = _inproj(x2d, p['norm_g'], p['w_in'], BF16)
        proj = proj2d.reshape(B, L, D_PROJ)
        tail = _inproj(x2d.reshape(B, L, D)[:, L - SUBLANE:, :].reshape(B * SUBLANE, D),
                       p['norm_g'], p['w_in'], F32).reshape(B, SUBLANE, D_PROJ)

        yg, s_gla_t = _gla(proj, p['wg_pad'], p['bg'], p['ng'],
                           jnp.swapaxes(gla0[l], -1, -2), C)

        conv0_pad = jnp.concatenate(
            [jnp.zeros((B, SUBLANE - (CONV_WIDTH - 1), LRU_WIDTH), F32), lru_conv0[l]], axis=1)
        yl, h_last = _lru(proj, conv0_pad, lru_h0[l][:, None, :], p['cw'], p['cb'], p['wax'],
                          p['ba'], p['bx'], p['lam'])

        sh = shift0[l]
        yr, s_rw_bd = _rwkv(proj, sh[:, None, :3 * RWKV_WIDTH], sh[:, None, 3 * RWKV_WIDTH:],
                            _pack_rwkv_state(rwkv0[l]),
                            p['mu_rkv'], p['mu_wa'], p['w0'], p['wup_pad'], p['a0'], p['aup_pad'],
                            p['k_k'], p['k_a'], p['r_k'], p['gn_w'], p['gn_b'], C)

        x2d = _outproj(x2d, yg.reshape(T, D), yl.reshape(T, D), yr.reshape(T, D), proj2d,
                       p['wpg'], p['wpl'], p['wpr'], p['wo'], final_g, final=(l == DEPTH - 1))

        n_gla.append(jnp.swapaxes(s_gla_t, -1, -2))
        n_h.append(h_last[:, 0, :])
        n_conv.append(tail[:, SUBLANE - (CONV_WIDTH - 1):, COL_XL:COL_XL + LRU_WIDTH])
        n_rw.append(_unpack_rwkv_state(s_rw_bd))
        n_shift.append(jnp.concatenate(
            [tail[:, -1, COL_R:COL_R + 3 * RWKV_WIDTH], tail[:, -1, COL_WA:COL_WA + 2 * DECAY_RANK]],
            axis=-1))
    return (x2d.reshape(B, L, D), jnp.stack(n_gla), jnp.stack(n_h), jnp.stack(n_conv),
            jnp.stack(n_rw), jnp.stack(n_shift))


def kernel(x_prompt, x_sample, state_gla, state_lru_h, state_lru_conv, state_rwkv, state_rwkv_shift,
           norm_g, w_in, gla_w_gup, gla_b_g, gla_norm_g,
           lru_conv_w, lru_conv_b, lru_w_a, lru_b_a, lru_w_x, lru_b_x, lru_lambda,
           rwkv_mu, rwkv_w0, rwkv_w_up, rwkv_a0, rwkv_a_up, rwkv_k_k, rwkv_k_a, rwkv_r_k,
           rwkv_gn_w, rwkv_gn_b, w_proj_gla, w_proj_lru, w_proj_rwkv, w_out, final_norm_g):
    P = dict(norm_g=norm_g, w_in=w_in, gla_w_gup=gla_w_gup, gla_b_g=gla_b_g, gla_norm_g=gla_norm_g,
             lru_conv_w=lru_conv_w, lru_conv_b=lru_conv_b, lru_w_a=lru_w_a, lru_b_a=lru_b_a,
             lru_w_x=lru_w_x, lru_b_x=lru_b_x, lru_lambda=lru_lambda,
             rwkv_mu=rwkv_mu, rwkv_w0=rwkv_w0, rwkv_w_up=rwkv_w_up, rwkv_a0=rwkv_a0,
             rwkv_a_up=rwkv_a_up, rwkv_k_k=rwkv_k_k, rwkv_k_a=rwkv_k_a, rwkv_r_k=rwkv_r_k,
             rwkv_gn_w=rwkv_gn_w, rwkv_gn_b=rwkv_gn_b, w_proj_gla=w_proj_gla, w_proj_lru=w_proj_lru,
             w_proj_rwkv=w_proj_rwkv, w_out=w_out)
    layers = [_prep_layer(P, l) for l in range(DEPTH)]
    final_g = final_norm_g.reshape(1, -1)
    dt = x_prompt.dtype
    B = x_prompt.shape[0]
    zeros = lambda *s: jnp.zeros((DEPTH, B) + s, dt)
    out_p = _run_trunk(x_prompt, zeros(GLA_HEADS, GLA_HK, GLA_HV), zeros(LRU_WIDTH),
                       zeros(CONV_WIDTH - 1, LRU_WIDTH), zeros(RWKV_HEADS, RWKV_HD, RWKV_HD),
                       zeros(SHIFT_WIDTH), layers, final_g, C=64)
    Ls = x_sample.shape[1]
    out_s = _run_trunk(x_sample, state_gla, state_lru_h, state_lru_conv, state_rwkv, state_rwkv_shift,
                       layers, final_g, C=64 if Ls % 64 == 0 else Ls)
    return (out_p[0], out_s[0]) + tuple(out_p[1:]) + tuple(out_s[1:])
```

```python
import functools

import numpy as np
import jax
import jax.numpy as jnp
from jax import lax
from jax.experimental import pallas as pl
from jax.experimental.pallas import tpu as pltpu

F32 = jnp.float32
BF16 = jnp.bfloat16

D_MODEL = 1024
DEPTH = 4
NORM_EPS = 1e-6
GLA_HEADS = 4
GLA_HK = 128
GLA_HV = 256
GLA_DK = GLA_HEADS * GLA_HK
GLA_DV = GLA_HEADS * GLA_HV
GLA_RANK = 16
GLA_TAU = 16.0
GLA_HEADS_PER_STEP = 4
LRU_WIDTH = 1024
LRU_BLOCKS = 8
LRU_BS = 128
CONV_WIDTH = 4
LRU_C = 8.0
RWKV_WIDTH = 1024
RWKV_HD = 64
RWKV_HEADS = 16
RWKV_GROUP = 2
RWKV_GW = RWKV_GROUP * RWKV_HD
RWKV_NGROUPS = RWKV_HEADS // RWKV_GROUP
RWKV_LOCAL_CHUNKS = 2
DECAY_RANK = 64
AAA_RANK = 64
RWKV_GN_EPS = 64e-5
RWKV_DECAY_SCALE = 0.6065306597126334
SHIFT_WIDTH = 3 * RWKV_WIDTH + DECAY_RANK + AAA_RANK

LANE = 128
SUBLANE = 8

COL_Q = 0
COL_K = 512
COL_V = 1024
COL_ZGLA = 2048
COL_XL = 3072
COL_ZLRU = 4096
COL_R = 5120
COL_KRW = 6144
COL_VRW = 7168
COL_ZRW = 8192
COL_GATES = 9216
COL_GD = 12288
COL_WA = 12416
D_PROJ = 12544
PROJ_TN = 1792

VMEM_LIMIT = 56 * 1024 * 1024


def _cparams(sem):
    return pltpu.CompilerParams(dimension_semantics=sem, vmem_limit_bytes=VMEM_LIMIT)


def _dot(a, b, precision=None):
    return jnp.dot(a, b, preferred_element_type=F32, precision=precision)


def _dot_nt(a, b):
    return lax.dot_general(a, b, (((1,), (1,)), ((), ())), preferred_element_type=F32)


def _dot_tn(a, b):
    return lax.dot_general(a, b, (((0,), (0,)), ((), ())), preferred_element_type=F32)


def _sigmoid(x):
    return 1.0 / (1.0 + jnp.exp(-x))


def _silu(x):
    return x * _sigmoid(x)


def _softplus(x):
    return jnp.maximum(x, 0.0) + jnp.log1p(jnp.exp(-jnp.abs(x)))


def _iota2(shape, dim):
    return lax.broadcasted_iota(jnp.int32, shape, dim)


def _inproj_kernel(x_ref, g_ref, w_ref, o_ref, xn_ref):
    @pl.when(pl.program_id(1) == 0)
    def _():
        x = x_ref[...]
        y = x * lax.rsqrt(jnp.mean(x * x, axis=-1, keepdims=True) + NORM_EPS)
        xn_ref[...] = (y * g_ref[...]).astype(BF16)

    o_ref[...] = _dot(xn_ref[...], w_ref[...]).astype(o_ref.dtype)


def _inproj(x2d, g, w, out_dtype):
    T = x2d.shape[0]
    tm = min(T, 1024)
    return pl.pallas_call(
        _inproj_kernel,
        grid=(T // tm, D_PROJ // PROJ_TN),
        in_specs=[
            pl.BlockSpec((tm, D_MODEL), lambda i, j: (i, 0)),
            pl.BlockSpec((1, D_MODEL), lambda i, j: (0, 0)),
            pl.BlockSpec((D_MODEL, PROJ_TN), lambda i, j: (0, j)),
        ],
        out_specs=pl.BlockSpec((tm, PROJ_TN), lambda i, j: (i, j)),
        out_shape=jax.ShapeDtypeStruct((T, D_PROJ), out_dtype),
        scratch_shapes=[pltpu.VMEM((tm, D_MODEL), BF16)],
        compiler_params=_cparams(("parallel", "arbitrary")),
        name="inproj",
    )(x2d, g, w)


def _gla_levels(C):
    out, s = [], C // 2
    while s >= 1:
        out.append(s)
        s //= 2
    return out


def _gla_select_matrix(C):
    r = np.arange(C)
    tril = (r[:, None] >= r[None, :])
    blocks = [tril]
    for s in _gla_levels(C):
        boundary = (r // (2 * s)) * (2 * s) + s - 1
        blocks.append(tril[boundary])
    sel = np.concatenate(blocks, axis=0)
    return jnp.asarray(np.concatenate([sel, sel, sel], axis=1), dtype=BF16)


def _gla_kernel(q_ref, k_ref, v_ref, z_ref, gd_ref, wg_ref, bg_ref, ng_ref, s0_ref, gsel_ref,
                y_ref, s_ref, st_ref, *, C):
    li = pl.program_id(2)
    HG = GLA_HEADS_PER_STEP

    @pl.when(li == 0)
    def _():
        st_ref[...] = s0_ref[0]

    row = _iota2((C, C), 0)
    col = _iota2((C, C), 1)
    eye = row == col
    masks = [((row // (2 * s)) == (col // (2 * s))) & ((row % (2 * s)) >= s) & ((col % (2 * s)) < s)
             for s in _gla_levels(C)]
    gsel = gsel_ref[...]
    TL = q_ref.shape[1]
    probs = [(h, slice(c * C, (c + 1) * C)) for h in range(HG) for c in range(TL // C)]
    kcols = lambda h: slice(h * GLA_HK, (h + 1) * GLA_HK)
    vcols = lambda h: slice(h * GLA_HV, (h + 1) * GLA_HV)

    def stage_decay(prob):
        h, sl = prob
        x = _dot(gd_ref[0, sl, :], wg_ref[:, kcols(h)]) + bg_ref[:, kcols(h)]
        la = (jnp.minimum(x, 0.0) - jnp.log1p(jnp.exp(-jnp.abs(x)))) * (1.0 / GLA_TAU)
        return dict(h=h, sl=sl, d_all=_cumsum_rows(gsel, la))

    def stage_scores(p):
        h, sl = p['h'], p['sl']
        q = q_ref[0, sl, kcols(h)].astype(F32) * (GLA_HK ** -0.5)
        k = k_ref[0, sl, kcols(h)].astype(F32)
        d_all = p['d_all']
        b = d_all[:C]
        bend = b[C - 1:C, :]
        a = jnp.where(eye, jnp.sum(q * k, axis=-1, keepdims=True), 0.0)
        for lvl, mask in enumerate(masks):
            d = b - d_all[(lvl + 1) * C:(lvl + 2) * C]
            qs = q * jnp.exp(jnp.minimum(d, 0.0))
            ks = k * jnp.exp(jnp.minimum(-d, 0.0))
            a = jnp.where(mask, _dot_nt(qs.astype(BF16), ks.astype(BF16)), a)
        return dict(h=h, sl=sl, a=a.astype(BF16), qd=(q * jnp.exp(b)).astype(BF16),
                    kd=(k * jnp.exp(bend - b)).astype(BF16), wend=jnp.exp(bend))

    def stage_values(p):
        v = v_ref[0, p['sl'], vcols(p['h'])]
        p.update(o=_dot(p['a'], v), upd=_dot_tn(v, p['kd']))
        return p

    ps = [stage_decay(prob) for prob in probs]
    ps = [stage_scores(p) for p in ps]
    ps = [stage_values(p) for p in ps]

    sts = [st_ref[h] for h in range(HG)]
    for p in ps:
        h, sl = p['h'], p['sl']
        o = p['o'] + _dot_nt(p['qd'], sts[h].astype(BF16))
        sts[h] = sts[h] * p['wend'] + p['upd']
        o = o * lax.rsqrt(jnp.mean(o * o, axis=-1, keepdims=True) + NORM_EPS) * ng_ref[:, vcols(h)]
        y_ref[0, sl, vcols(h)] = (o * _silu(z_ref[0, sl, vcols(h)].astype(F32))).astype(BF16)
    for h in range(HG):
        st_ref[h] = sts[h]

    @pl.when(li == pl.num_programs(2) - 1)
    def _():
        for h in range(HG):
            s_ref[0, h] = sts[h]


def _gla(proj, wg_pad, bg, ng, s0t, C):
    B, L, _ = proj.shape
    TL = min(L, 256)
    HG = GLA_HEADS_PER_STEP
    KW, VW = HG * GLA_HK, HG * GLA_HV
    qb, kb = COL_Q // KW, COL_K // KW
    vb, zb = COL_V // VW, COL_ZGLA // VW
    gb = COL_GD // LANE
    gsel = _gla_select_matrix(C)
    st_spec = pl.BlockSpec((1, HG, GLA_HV, GLA_HK), lambda b, h, l: (b, h, 0, 0))
    return pl.pallas_call(
        functools.partial(_gla_kernel, C=C),
        grid=(B, GLA_HEADS // HG, L // TL),
        in_specs=[
            pl.BlockSpec((1, TL, KW), lambda b, h, l: (b, l, qb + h)),
            pl.BlockSpec((1, TL, KW), lambda b, h, l: (b, l, kb + h)),
            pl.BlockSpec((1, TL, VW), lambda b, h, l: (b, l, vb + h)),
            pl.BlockSpec((1, TL, VW), lambda b, h, l: (b, l, zb + h)),
            pl.BlockSpec((1, TL, LANE), lambda b, h, l: (b, l, gb)),
            pl.BlockSpec((LANE, KW), lambda b, h, l: (0, h)),
            pl.BlockSpec((1, KW), lambda b, h, l: (0, h)),
            pl.BlockSpec((1, VW), lambda b, h, l: (0, h)),
            st_spec,
            pl.BlockSpec(gsel.shape, lambda b, h, l: (0, 0)),
        ],
        out_specs=[
            pl.BlockSpec((1, TL, VW), lambda b, h, l: (b, l, h)),
            st_spec,
        ],
        out_shape=[
            jax.ShapeDtypeStruct((B, L, GLA_DV), BF16),
            jax.ShapeDtypeStruct((B, GLA_HEADS, GLA_HV, GLA_HK), F32),
        ],
        scratch_shapes=[pltpu.VMEM((HG, GLA_HV, GLA_HK), F32)],
        compiler_params=_cparams(("parallel", "parallel", "arbitrary")),
        name="gla",
    )(proj, proj, proj, proj, proj, wg_pad, bg, ng, s0t, gsel)


def _lru_kernel(x_ref, z_ref, c0_ref, h0_ref, cw_ref, cb_ref, wax_ref, ba_ref, bx_ref, lam_ref,
                y_ref, hl_ref, xpad_ref, a_ref, u_ref, hs_ref, h_ref):
    li = pl.program_id(1)
    TL = x_ref.shape[1]
    PAD = SUBLANE

    @pl.when(li == 0)
    def _():
        xpad_ref[0:PAD, :] = c0_ref[0]
        h_ref[...] = h0_ref[0]

    xpad_ref[PAD:PAD + TL, :] = x_ref[0].astype(F32)
    xc = cb_ref[...]
    for t in range(CONV_WIDTH):
        off = PAD - (CONV_WIDTH - 1) + t
        xc = xc + xpad_ref[off:off + TL, :] * cw_ref[t:t + 1, :]
    xc_b = xc.astype(BF16)
    rs, xs = [], []
    for n in range(LRU_BLOCKS):
        g = _dot(xc_b[:, n * LRU_BS:(n + 1) * LRU_BS], wax_ref[n])
        rs.append(g[:, :LRU_BS])
        xs.append(g[:, LRU_BS:])
    r = _sigmoid(jnp.concatenate(rs, axis=1) + ba_ref[...])
    i = _sigmoid(jnp.concatenate(xs, axis=1) + bx_ref[...])
    log_a = (-LRU_C) * r * _softplus(-lam_ref[...])
    a_ref[...] = jnp.exp(log_a)
    th = jnp.tanh(log_a)
    u_ref[...] = jnp.sqrt(-2.0 * th / (1.0 - th)) * (i * xc)

    row8 = _iota2((SUBLANE, LRU_WIDTH), 0)

    def body(g, h):
        base = pl.multiple_of(g * SUBLANE, SUBLANE)
        a = a_ref[pl.ds(base, SUBLANE), :]
        u = u_ref[pl.ds(base, SUBLANE), :]
        s = 1
        while s < SUBLANE:
            keep = row8 >= s
            u = u + a * jnp.where(keep, pltpu.roll(u, s, axis=0), 0.0)
            a = a * jnp.where(keep, pltpu.roll(a, s, axis=0), 1.0)
            s *= 2
        hs = a * h + u
        hs_ref[pl.ds(base, SUBLANE), :] = hs
        return hs[SUBLANE - 1:SUBLANE, :]

    h = lax.fori_loop(0, TL // SUBLANE, body, h_ref[...], unroll=2)
    h_ref[...] = h
    hl_ref[0] = h
    y_ref[0] = (hs_ref[...] * _silu(z_ref[0].astype(F32))).astype(BF16)
    xpad_ref[0:PAD, :] = xpad_ref[TL:TL + PAD, :]


def _lru(proj, conv0_pad, h0, cw, cb, wax, ba, bx, lam):
    B, L, _ = proj.shape
    TL = min(L, 256)
    W = LRU_WIDTH
    xb, zb = COL_XL // W, COL_ZLRU // W
    vec = pl.BlockSpec((1, W), lambda b, l: (0, 0))
    return pl.pallas_call(
        _lru_kernel,
        grid=(B, L // TL),
        in_specs=[
            pl.BlockSpec((1, TL, W), lambda b, l: (b, l, xb)),
            pl.BlockSpec((1, TL, W), lambda b, l: (b, l, zb)),
            pl.BlockSpec((1, SUBLANE, W), lambda b, l: (b, 0, 0)),
            pl.BlockSpec((1, 1, W), lambda b, l: (b, 0, 0)),
            pl.BlockSpec((CONV_WIDTH, W), lambda b, l: (0, 0)),
            vec,
            pl.BlockSpec((LRU_BLOCKS, LRU_BS, 2 * LRU_BS), lambda b, l: (0, 0, 0)),
            vec, vec, vec,
        ],
        out_specs=[
            pl.BlockSpec((1, TL, W), lambda b, l: (b, l, 0)),
            pl.BlockSpec((1, 1, W), lambda b, l: (b, 0, 0)),
        ],
        out_shape=[
            jax.ShapeDtypeStruct((B, L, W), BF16),
            jax.ShapeDtypeStruct((B, 1, W), F32),
        ],
        scratch_shapes=[
            pltpu.VMEM((TL + SUBLANE, W), F32),
            pltpu.VMEM((TL, W), F32),
            pltpu.VMEM((TL, W), F32),
            pltpu.VMEM((TL, W), F32),
            pltpu.VMEM((1, W), F32),
        ],
        compiler_params=_cparams(("parallel", "arbitrary")),
        name="rglru",
    )(proj, proj, conv0_pad, h0, cw, cb, wax, ba, bx, lam)


def _headsum(x, ones_bd):
    xb = x.astype(BF16)
    outs = [_dot(xb[:, g * RWKV_GW:(g + 1) * RWKV_GW], ones_bd) for g in range(RWKV_NGROUPS)]
    return jnp.concatenate(outs, axis=1)


def _cumsum_rows(sel3, x):
    hi = x.astype(BF16)
    r1 = x - hi.astype(F32)
    mid = r1.astype(BF16)
    lo = (r1 - mid.astype(F32)).astype(BF16)
    return _dot(sel3, jnp.concatenate([hi, mid, lo], axis=0))


def _rwkv_local(loads, C, masks):
    tril_b, strict_rb, incl_rb, eye_rb, bd_sq, bd_cv, bd_state = masks

    def tile_rows(x, n):
        return jnp.concatenate([x] * n, axis=0)

    def blockdiag_cv(x):
        return tile_rows(x.astype(BF16), RWKV_GROUP) * bd_cv

    def blockdiag_sq(x):
        return tile_rows(x.astype(BF16), RWKV_GROUP) * bd_sq

    def stage_decay(load):
        r, k, v, kk, a, lw = load()
        lc = _cumsum_rows(tril_b, lw)
        wend = jnp.exp(lc[C - 1:C, :])
        einv = jnp.exp(-lc)
        bt = kk * jnp.exp(lc - lw)
        rt = r * jnp.exp(lc)
        at = -(kk * a) * einv
        kt = k * einv
        return dict(v=v, wend=wend, bt=bt, rt=rt, at=at, kt=kt)

    def stage_scores(p):
        br = jnp.concatenate([p['bt'], p['rt']], axis=0).astype(BF16)
        pa = _dot_nt(br, blockdiag_cv(p['at']))
        pk = _dot_nt(br, blockdiag_cv(p['kt']))
        a_ba = jnp.where(strict_rb, pa[:C], 0.0)
        p.update(a_ra=jnp.where(incl_rb, pa[C:], 0.0).astype(BF16),
                 a_bk=jnp.where(strict_rb, pk[:C], 0.0).astype(BF16),
                 a_rk=jnp.where(incl_rb, pk[C:], 0.0).astype(BF16),
                 x=a_ba, xb=blockdiag_sq(a_ba), t=eye_rb + a_ba)
        return p

    def stage_square(p):
        p['x'] = _dot(p['x'].astype(BF16), p['xb'])
        p['xb'] = blockdiag_sq(p['x'])
        return p

    def stage_double(p, last):
        if last:
            p['t'] = p['t'] + _dot(p['t'].astype(BF16), p['xb'])
            return p
        xt = _dot(jnp.concatenate([p['x'], p['t']], axis=0).astype(BF16), p['xb'])
        p['x'] = xt[:C]
        p['t'] = p['t'] + xt[C:]
        p['xb'] = blockdiag_sq(p['x'])
        return p

    def stage_apply(p):
        tb = p['t'].astype(BF16)
        v_bd = blockdiag_cv(p['v'])
        xv = _dot(jnp.concatenate([p['a_bk'], p['a_rk']], axis=0), v_bd)
        p.update(yv=xv[C:], bhat=_dot(tb, blockdiag_cv(p['bt'])),
                 uloc=_dot(tb, blockdiag_cv(xv[:C])))
        return p

    def stage_out(p):
        rhat = p['rt'] + _dot(p['a_ra'], blockdiag_cv(p['bhat']))
        yloc = _dot(p['a_ra'], blockdiag_cv(p['uloc'])) + p['yv']
        atw = (p['at'] * p['wend']).astype(BF16)
        ktw = (p['kt'] * p['wend']).astype(BF16)
        ghat = jnp.where(bd_state, _dot_tn(p['bhat'].astype(BF16), atw), 0.0)
        h = jnp.where(bd_state,
                      _dot_tn(jnp.concatenate([p['uloc'], p['v']], axis=0).astype(BF16),
                              jnp.concatenate([atw, ktw], axis=0)), 0.0)
        return rhat, yloc, ghat, h, p['wend']

    ps = [stage_decay(ld) for ld in loads]
    ps = [stage_scores(p) for p in ps]
    ps = [stage_square(p) for p in ps]
    n = 4
    while n <= C:
        ps = [stage_double(p, last=(n == C)) for p in ps]
        n *= 2
    ps = [stage_apply(p) for p in ps]
    return [stage_out(p) for p in ps]


def _rwkv_masks(C):
    G = RWKV_GROUP
    row = _iota2((C, C), 0)
    col = _iota2((C, C), 1)
    tril = (_iota2((C, 3 * C), 0) >= (_iota2((C, 3 * C), 1) % C)).astype(BF16)
    r_rb = _iota2((C, G * C), 0)
    c_rb = _iota2((C, G * C), 1) % C
    strict_rb = r_rb > c_rb
    incl_rb = r_rb >= c_rb
    eye_rb = (r_rb == c_rb).astype(F32)
    bd_sq = ((_iota2((G * C, G * C), 0) // C) == (_iota2((G * C, G * C), 1) // C)).astype(BF16)
    bd_cv = ((_iota2((G * C, RWKV_GW), 0) // C) == (_iota2((G * C, RWKV_GW), 1) // RWKV_HD)).astype(BF16)
    bd_state = (_iota2((RWKV_GW, RWKV_GW), 0) // RWKV_HD) == (_iota2((RWKV_GW, RWKV_GW), 1) // RWKV_HD)
    return tril, strict_rb, incl_rb, eye_rb, bd_sq, bd_cv, bd_state


def _rwkv_kernel(r_ref, k_ref, v_ref, wa_ref, z_ref, sh_rkv_ref, sh_wa_ref, s0_ref,
                 mu_rkv_ref, mu_wa_ref, w0_ref, wup_ref, a0_ref, aup_ref, kk_ref, ka_ref, rk_ref,
                 gnw_ref, gnb_ref,
                 y_ref, s_ref,
                 xpad_ref, wapad_ref, st_ref, rs_ref, ks_ref, vs_ref, kks_ref, as_ref, lws_ref, ys_ref,
                 rhat_ref, ghat_ref, h_ref, wend_ref, *, C):
    li = pl.program_id(1)
    TL = r_ref.shape[1]
    PAD = SUBLANE
    W = RWKV_WIDTH

    @pl.when(li == 0)
    def _():
        for n in range(3):
            xpad_ref[n, PAD - 1:PAD, :] = sh_rkv_ref[0, :, n * W:(n + 1) * W]
        wapad_ref[PAD - 1:PAD, :] = sh_wa_ref[0]
        for g in range(RWKV_NGROUPS):
            st_ref[g] = s0_ref[0, g]

    mixed = []
    for n, ref in enumerate((r_ref, k_ref, v_ref)):
        x = ref[0].astype(F32)
        xpad_ref[n, PAD:PAD + TL, :] = x
        prev = xpad_ref[n, PAD - 1:PAD - 1 + TL, :]
        mixed.append(x + (prev - x) * mu_rkv_ref[:, n * W:(n + 1) * W])
        xpad_ref[n, PAD - 1:PAD, :] = xpad_ref[n, PAD + TL - 1:PAD + TL, :]
    r, k, v = mixed
    xwa = wa_ref[0].astype(F32)
    wapad_ref[PAD:PAD + TL, :] = xwa
    prev = wapad_ref[PAD - 1:PAD - 1 + TL, :]
    wapad_ref[PAD - 1:PAD, :] = wapad_ref[PAD + TL - 1:PAD + TL, :]
    xwa = xwa + (prev - xwa) * mu_wa_ref[...]

    w = w0_ref[...] + _dot(jnp.tanh(xwa).astype(BF16), wup_ref[...])
    lw = (-RWKV_DECAY_SCALE) * _sigmoid(w)
    a = _sigmoid(a0_ref[...] + _dot(xwa.astype(BF16), aup_ref[...]))

    ones_bd = ((_iota2((RWKV_GW, RWKV_GW), 0) // RWKV_HD)
               == (_iota2((RWKV_GW, RWKV_GW), 1) // RWKV_HD)).astype(BF16)
    kk = k * kk_ref[...]
    kk = kk * lax.rsqrt(_headsum(kk * kk, ones_bd) + 1e-12)
    k = k * (1.0 + (a - 1.0) * ka_ref[...])
    rs_ref[...] = r
    ks_ref[...] = k
    vs_ref[...] = v
    kks_ref[...] = kk
    as_ref[...] = a
    lws_ref[...] = lw

    masks = _rwkv_masks(C)

    NCH = TL // C
    NB = min(NCH, RWKV_LOCAL_CHUNKS)

    def loader(rows, cols):
        return lambda: tuple(ref[rows, cols]
                             for ref in (rs_ref, ks_ref, vs_ref, kks_ref, as_ref, lws_ref))

    for cb in range(NCH // NB):
        probs = []
        for c in range(cb * NB, (cb + 1) * NB):
            rows = slice(c * C, (c + 1) * C)
            for g in range(RWKV_NGROUPS):
                probs.append((c, rows, g, slice(g * RWKV_GW, (g + 1) * RWKV_GW)))
        outs = _rwkv_local([loader(rows, cols) for _, rows, _, cols in probs], C, masks)
        for (c, rows, g, cols), (rhat, yloc, ghat, h, wend) in zip(probs, outs):
            rhat_ref[rows, cols] = rhat.astype(BF16)
            ys_ref[rows, cols] = yloc
            ghat_ref[c, g] = ghat.astype(BF16)
            h_ref[c, g] = h
            wend_ref[c, :, cols] = wend

    states = [st_ref[g] for g in range(RWKV_NGROUPS)]
    for c in range(NCH):
        rows = slice(c * C, (c + 1) * C)
        for g in range(RWKV_NGROUPS):
            cols = slice(g * RWKV_GW, (g + 1) * RWKV_GW)
            sb = states[g].astype(BF16)
            ys_ref[rows, cols] = ys_ref[rows, cols] + _dot_nt(rhat_ref[rows, cols], sb)
            states[g] = states[g] * wend_ref[c, :, cols] + _dot(sb, ghat_ref[c, g]) + h_ref[c, g]
    for g in range(RWKV_NGROUPS):
        st_ref[g] = states[g]

    y = ys_ref[...]
    r = rs_ref[...]
    k = ks_ref[...]
    v = vs_ref[...]
    inv_hd = 1.0 / RWKV_HD
    mean = _headsum(y, ones_bd) * inv_hd
    d = y - mean
    var = _headsum(d * d, ones_bd) * inv_hd
    yn = d * lax.rsqrt(var + RWKV_GN_EPS) * gnw_ref[...] + gnb_ref[...]
    yn = yn + _headsum(r * k * rk_ref[...], ones_bd) * v
    y_ref[0] = (yn * _silu(z_ref[0].astype(F32))).astype(BF16)

    @pl.when(li == pl.num_programs(1) - 1)
    def _():
        for g in range(RWKV_NGROUPS):
            s_ref[0, g] = st_ref[g]


def _rwkv(proj, sh_rkv, sh_wa, s0_bd, mu_rkv, mu_wa, w0, wup_pad, a0, aup_pad, k_k, k_a, r_k, gn_w, gn_b, C):
    B, L, _ = proj.shape
    TL = min(L, 256)
    W = RWKV_WIDTH
    rb, kb, vb, zb = COL_R // W, COL_KRW // W, COL_VRW // W, COL_ZRW // W
    wab = COL_WA // LANE
    vec = pl.BlockSpec((1, W), lambda b, l: (0, 0))
    mat = pl.BlockSpec((LANE, W), lambda b, l: (0, 0))
    tok = lambda cb: pl.BlockSpec((1, TL, W), lambda b, l: (b, l, cb))
    st_spec = pl.BlockSpec((1, RWKV_NGROUPS, RWKV_GW, RWKV_GW), lambda b, l: (b, 0, 0, 0))
    return pl.pallas_call(
        functools.partial(_rwkv_kernel, C=C),
        grid=(B, L // TL),
        in_specs=[
            tok(rb), tok(kb), tok(vb),
            pl.BlockSpec((1, TL, LANE), lambda b, l: (b, l, wab)),
            tok(zb),
            pl.BlockSpec((1, 1, 3 * W), lambda b, l: (b, 0, 0)),
            pl.BlockSpec((1, 1, LANE), lambda b, l: (b, 0, 0)),
            st_spec,
            pl.BlockSpec((1, 3 * W), lambda b, l: (0, 0)),
            pl.BlockSpec((1, LANE), lambda b, l: (0, 0)),
            vec, mat, vec, mat, vec, vec, vec, vec, vec,
        ],
        out_specs=[
            pl.BlockSpec((1, TL, W), lambda b, l: (b, l, 0)),
            st_spec,
        ],
        out_shape=[
            jax.ShapeDtypeStruct((B, L, W), BF16),
            jax.ShapeDtypeStruct((B, RWKV_NGROUPS, RWKV_GW, RWKV_GW), F32),
        ],
        scratch_shapes=[
            pltpu.VMEM((3, TL + SUBLANE, W), F32),
            pltpu.VMEM((TL + SUBLANE, LANE), F32),
            pltpu.VMEM((RWKV_NGROUPS, RWKV_GW, RWKV_GW), F32),
        ] + [pltpu.VMEM((TL, W), F32)] * 7 + [
            pltpu.VMEM((TL, W), BF16),
            pltpu.VMEM((TL // C, RWKV_NGROUPS, RWKV_GW, RWKV_GW), BF16),
            pltpu.VMEM((TL // C, RWKV_NGROUPS, RWKV_GW, RWKV_GW), F32),
            pltpu.VMEM((TL // C, 1, W), F32),
        ],
        compiler_params=_cparams(("parallel", "arbitrary")),
        name="rwkv7",
    )(proj, proj, proj, proj, proj, sh_rkv, sh_wa, s0_bd,
      mu_rkv, mu_wa, w0, wup_pad, a0, aup_pad, k_k, k_a, r_k, gn_w, gn_b)


def _outproj_kernel(x_ref, yg_ref, yl_ref, yr_ref, ga_ref, gb_ref, gc_ref,
                    wg_ref, wl_ref, wr_ref, wo_ref, fg_ref, o_ref, *, final):
    merged = (_sigmoid(ga_ref[...].astype(F32)) * _dot(yg_ref[...], wg_ref[...])
              + _sigmoid(gb_ref[...].astype(F32)) * _dot(yl_ref[...], wl_ref[...])
              + _sigmoid(gc_ref[...].astype(F32)) * _dot(yr_ref[...], wr_ref[...]))
    x = x_ref[...] + _dot(merged.astype(BF16), wo_ref[...])
    if final:
        x = x * lax.rsqrt(jnp.mean(x * x, axis=-1, keepdims=True) + NORM_EPS) * fg_ref[...]
    o_ref[...] = x


def _outproj(x2d, yg, yl, yr, proj2d, wg, wl, wr, wo, fg, final):
    T = x2d.shape[0]
    tm = min(T, 256)
    D = D_MODEL
    gb = COL_GATES // D
    tok = lambda cb: pl.BlockSpec((tm, D), lambda i: (i, cb))
    wspec = pl.BlockSpec((D, D), lambda i: (0, 0))
    return pl.pallas_call(
        functools.partial(_outproj_kernel, final=final),
        grid=(T // tm,),
        in_specs=[tok(0), tok(0), tok(0), tok(0), tok(gb), tok(gb + 1), tok(gb + 2),
                  wspec, wspec, wspec, wspec, pl.BlockSpec((1, D), lambda i: (0, 0))],
        out_specs=tok(0),
        out_shape=jax.ShapeDtypeStruct((T, D), F32),
        compiler_params=_cparams(("parallel",)),
        name="outproj",
    )(x2d, yg, yl, yr, proj2d, proj2d, proj2d, wg, wl, wr, wo, fg)


def _prep_layer(P, l):
    w_in = P['w_in'][l]
    o_gd = GLA_DK + GLA_DK + GLA_DV
    o_zg = o_gd + GLA_RANK
    o_xl = o_zg + GLA_DV
    o_zl = o_xl + LRU_WIDTH
    o_rw = o_zl + LRU_WIDTH
    o_wa = o_rw + 3 * RWKV_WIDTH
    o_zr = o_rw + SHIFT_WIDTH
    o_gt = o_zr + RWKV_WIDTH
    w = jnp.concatenate([
        w_in[:, :o_gd],
        w_in[:, o_zg:o_rw],
        w_in[:, o_rw:o_wa],
        w_in[:, o_zr:],
        w_in[:, o_gd:o_zg],
        jnp.zeros((D_MODEL, LANE - GLA_RANK), F32),
        w_in[:, o_wa:o_zr],
    ], axis=1).astype(BF16)
    zpad = jnp.zeros((LANE - GLA_RANK, GLA_DK), F32)
    z64 = jnp.zeros((DECAY_RANK, RWKV_WIDTH), F32)
    mu = P['rwkv_mu'][l]
    row = lambda a: a.reshape(1, -1)
    return dict(
        norm_g=row(P['norm_g'][l]), w_in=w,
        wg_pad=jnp.concatenate([P['gla_w_gup'][l], zpad], axis=0).astype(BF16),
        bg=row(P['gla_b_g'][l]), ng=row(P['gla_norm_g'][l]),
        cw=P['lru_conv_w'][l], cb=row(P['lru_conv_b'][l]),
        wax=jnp.concatenate([P['lru_w_a'][l], P['lru_w_x'][l]], axis=-1).astype(BF16),
        ba=row(P['lru_b_a'][l]), bx=row(P['lru_b_x'][l]), lam=row(P['lru_lambda'][l]),
        mu_rkv=row(mu[:3 * RWKV_WIDTH]), mu_wa=row(mu[3 * RWKV_WIDTH:]),
        w0=row(P['rwkv_w0'][l]),
        wup_pad=jnp.concatenate([P['rwkv_w_up'][l], z64], axis=0).astype(BF16),
        a0=row(P['rwkv_a0'][l]),
        aup_pad=jnp.concatenate([z64, P['rwkv_a_up'][l]], axis=0).astype(BF16),
        k_k=row(P['rwkv_k_k'][l]), k_a=row(P['rwkv_k_a'][l]), r_k=row(P['rwkv_r_k'][l]),
        gn_w=row(P['rwkv_gn_w'][l]), gn_b=row(P['rwkv_gn_b'][l]),
        wpg=P['w_proj_gla'][l].astype(BF16), wpl=P['w_proj_lru'][l].astype(BF16),
        wpr=P['w_proj_rwkv'][l].astype(BF16), wo=P['w_out'][l].astype(BF16),
    )


def _pack_rwkv_state(s):
    N = s.shape[0]
    s = s.reshape(N, RWKV_NGROUPS, RWKV_GROUP, RWKV_HD, RWKV_HD)
    eye = jnp.eye(RWKV_GROUP, dtype=s.dtype)
    out = jnp.einsum('nghvk,hj->nghvjk', s, eye)
    return out.reshape(N, RWKV_NGROUPS, RWKV_GW, RWKV_GW)


def _unpack_rwkv_state(s_bd):
    N = s_bd.shape[0]
    s = s_bd.reshape(N, RWKV_NGROUPS, RWKV_GROUP, RWKV_HD, RWKV_GROUP, RWKV_HD)
    idx = jnp.arange(RWKV_GROUP)
    s = s[:, :, idx, :, idx, :]
    return jnp.moveaxis(s, 0, 2).reshape(N, RWKV_HEADS, RWKV_HD, RWKV_HD)


def _to_kernel_states(gla, lru_h, lru_conv, rwkv, shift):
    Bn = gla.shape[1]
    conv_pad = jnp.zeros((DEPTH, Bn, SUBLANE - (CONV_WIDTH - 1), LRU_WIDTH), lru_conv.dtype)
    rw_bd = _pack_rwkv_state(rwkv.reshape((DEPTH * Bn,) + rwkv.shape[2:]))
    return (jnp.swapaxes(gla, -1, -2), lru_h[:, :, None, :],
            jnp.concatenate([conv_pad, lru_conv], axis=2),
            rw_bd.reshape((DEPTH, Bn) + rw_bd.shape[1:]),
            shift[:, :, None, :3 * RWKV_WIDTH], shift[:, :, None, 3 * RWKV_WIDTH:])


def _zero_kernel_states(Bn, dt):
    z = lambda *shape: jnp.zeros((DEPTH, Bn) + shape, dt)
    return (z(GLA_HEADS, GLA_HV, GLA_HK), z(1, LRU_WIDTH), z(SUBLANE, LRU_WIDTH),
            z(RWKV_NGROUPS, RWKV_GW, RWKV_GW), z(1, 3 * RWKV_WIDTH), z(1, LANE))


def _run_trunk(x, states, layers, final_g, C):
    gla0t, lru_h0, conv0_pad, rwkv0_bd, sh_rkv, sh_wa = states
    B, L, D = x.shape
    T = B * L
    x2d = x.reshape(T, D)
    n_gla, n_h, n_conv, n_rw, n_shift = [], [], [], [], []
    for l in range(DEPTH):
        p = layers[l]
        proj2d = _inproj(x2d, p['norm_g'], p['w_in'], BF16)
        proj = proj2d.reshape(B, L, D_PROJ)
        tail = _inproj(x2d.reshape(B, L, D)[:, L - SUBLANE:, :].reshape(B * SUBLANE, D),
                       p['norm_g'], p['w_in'], F32).reshape(B, SUBLANE, D_PROJ)

        yg, s_gla_t = _gla(proj, p['wg_pad'], p['bg'], p['ng'], gla0t[l], C)
        yl, h_last = _lru(proj, conv0_pad[l], lru_h0[l], p['cw'], p['cb'], p['wax'],
                          p['ba'], p['bx'], p['lam'])
        yr, s_rw_bd = _rwkv(proj, sh_rkv[l], sh_wa[l], rwkv0_bd[l],
                            p['mu_rkv'], p['mu_wa'], p['w0'], p['wup_pad'], p['a0'], p['aup_pad'],
                            p['k_k'], p['k_a'], p['r_k'], p['gn_w'], p['gn_b'], C)

        x2d = _outproj(x2d, yg.reshape(T, D), yl.reshape(T, D), yr.reshape(T, D), proj2d,
                       p['wpg'], p['wpl'], p['wpr'], p['wo'], final_g, final=(l == DEPTH - 1))

        n_gla.append(s_gla_t)
        n_h.append(h_last)
        n_conv.append(tail[:, SUBLANE - (CONV_WIDTH - 1):, COL_XL:COL_XL + LRU_WIDTH])
        n_rw.append(s_rw_bd)
        n_shift.append(jnp.concatenate(
            [tail[:, -1, COL_R:COL_R + 3 * RWKV_WIDTH], tail[:, -1, COL_WA:COL_WA + 2 * DECAY_RANK]],
            axis=-1))
    rw = jnp.stack(n_rw)
    rw = _unpack_rwkv_state(rw.reshape((DEPTH * B,) + rw.shape[2:]))
    return (x2d.reshape(B, L, D), jnp.swapaxes(jnp.stack(n_gla), -1, -2), jnp.stack(n_h)[:, :, 0, :],
            jnp.stack(n_conv), rw.reshape((DEPTH, B) + rw.shape[1:]), jnp.stack(n_shift))


def kernel(x_prompt, x_sample, state_gla, state_lru_h, state_lru_conv, state_rwkv, state_rwkv_shift,
           norm_g, w_in, gla_w_gup, gla_b_g, gla_norm_g,
           lru_conv_w, lru_conv_b, lru_w_a, lru_b_a, lru_w_x, lru_b_x, lru_lambda,
           rwkv_mu, rwkv_w0, rwkv_w_up, rwkv_a0, rwkv_a_up, rwkv_k_k, rwkv_k_a, rwkv_r_k,
           rwkv_gn_w, rwkv_gn_b, w_proj_gla, w_proj_lru, w_proj_rwkv, w_out, final_norm_g):
    P = dict(norm_g=norm_g, w_in=w_in, gla_w_gup=gla_w_gup, gla_b_g=gla_b_g, gla_norm_g=gla_norm_g,
             lru_conv_w=lru_conv_w, lru_conv_b=lru_conv_b, lru_w_a=lru_w_a, lru_b_a=lru_b_a,
             lru_w_x=lru_w_x, lru_b_x=lru_b_x, lru_lambda=lru_lambda,
             rwkv_mu=rwkv_mu, rwkv_w0=rwkv_w0, rwkv_w_up=rwkv_w_up, rwkv_a0=rwkv_a0,
             rwkv_a_up=rwkv_a_up, rwkv_k_k=rwkv_k_k, rwkv_k_a=rwkv_k_a, rwkv_r_k=rwkv_r_k,
             rwkv_gn_w=rwkv_gn_w, rwkv_gn_b=rwkv_gn_b, w_proj_gla=w_proj_gla, w_proj_lru=w_proj_lru,
             w_proj_rwkv=w_proj_rwkv, w_out=w_out)
    layers = [_prep_layer(P, l) for l in range(DEPTH)]
    final_g = final_norm_g.reshape(1, -1)
    out_p = _run_trunk(x_prompt, _zero_kernel_states(x_prompt.shape[0], x_prompt.dtype),
                       layers, final_g, C=64)
    Ls = x_sample.shape[1]
    out_s = _run_trunk(x_sample, _to_kernel_states(state_gla, state_lru_h, state_lru_conv,
                                                   state_rwkv, state_rwkv_shift),
                       layers, final_g, C=64 if Ls % 64 == 0 else Ls)
    return (out_p[0], out_s[0]) + tuple(out_p[1:]) + tuple(out_s[1:])
```

```python
import functools

import numpy as np
import jax
import jax.numpy as jnp
from jax import lax
from jax.experimental import pallas as pl
from jax.experimental.pallas import tpu as pltpu

F32 = jnp.float32
BF16 = jnp.bfloat16

D_MODEL = 1024
DEPTH = 4
NORM_EPS = 1e-6
GLA_HEADS = 4
GLA_HK = 128
GLA_HV = 256
GLA_DK = GLA_HEADS * GLA_HK
GLA_DV = GLA_HEADS * GLA_HV
GLA_RANK = 16
GLA_TAU = 16.0
GLA_HEADS_PER_STEP = 4
LRU_WIDTH = 1024
LRU_BLOCKS = 8
LRU_BS = 128
CONV_WIDTH = 4
LRU_C = 8.0
RWKV_WIDTH = 1024
RWKV_HD = 64
RWKV_HEADS = 16
RWKV_GROUP = 2
RWKV_GW = RWKV_GROUP * RWKV_HD
RWKV_NGROUPS = RWKV_HEADS // RWKV_GROUP
RWKV_LOCAL_CHUNKS = 2
DECAY_RANK = 64
AAA_RANK = 64
RWKV_GN_EPS = 64e-5
RWKV_DECAY_SCALE = 0.6065306597126334
SHIFT_WIDTH = 3 * RWKV_WIDTH + DECAY_RANK + AAA_RANK

LANE = 128
SUBLANE = 8

COL_Q = 0
COL_K = 512
COL_V = 1024
COL_ZGLA = 2048
COL_XL = 3072
COL_ZLRU = 4096
COL_R = 5120
COL_KRW = 6144
COL_VRW = 7168
COL_ZRW = 8192
COL_GATES = 9216
COL_GD = 12288
COL_WA = 12416
D_PROJ = 12544
PROJ_TN = 1792
INPROJ_TM = 1024
OUTPROJ_TM = 256
MIXER_TL = 256

VMEM_LIMIT = 56 * 1024 * 1024


def _cparams(sem):
    return pltpu.CompilerParams(dimension_semantics=sem, vmem_limit_bytes=VMEM_LIMIT)


def _dot(a, b):
    return jnp.dot(a, b, preferred_element_type=F32)


def _dot_nt(a, b):
    return lax.dot_general(a, b, (((1,), (1,)), ((), ())), preferred_element_type=F32)


def _dot_tn(a, b):
    return lax.dot_general(a, b, (((0,), (0,)), ((), ())), preferred_element_type=F32)


def _sigmoid(x):
    return 1.0 / (1.0 + jnp.exp(-x))


def _silu(x):
    return x * _sigmoid(x)


def _softplus(x):
    return jnp.maximum(x, 0.0) + jnp.log1p(jnp.exp(-jnp.abs(x)))


def _iota2(shape, dim):
    return lax.broadcasted_iota(jnp.int32, shape, dim)


def _cumsum_rows(sel3, x):
    hi = x.astype(BF16)
    r1 = x - hi.astype(F32)
    mid = r1.astype(BF16)
    lo = (r1 - mid.astype(F32)).astype(BF16)
    return _dot(sel3, jnp.concatenate([hi, mid, lo], axis=0))


def _inproj_kernel(x_ref, g_ref, w_ref, o_ref, tail_ref, xn_ref):
    @pl.when(pl.program_id(1) == 0)
    def _():
        x = x_ref[...]
        y = x * lax.rsqrt(jnp.mean(x * x, axis=-1, keepdims=True) + NORM_EPS)
        xn_ref[...] = (y * g_ref[...]).astype(BF16)

    acc = _dot(xn_ref[...], w_ref[...])
    o_ref[...] = acc.astype(o_ref.dtype)
    tail_ref[...] = acc[acc.shape[0] - SUBLANE:, :]


def _inproj(x2d, g, w, out_dtype):
    T = x2d.shape[0]
    tm = min(T, INPROJ_TM)
    return pl.pallas_call(
        _inproj_kernel,
        grid=(T // tm, D_PROJ // PROJ_TN),
        in_specs=[
            pl.BlockSpec((tm, D_MODEL), lambda i, j: (i, 0)),
            pl.BlockSpec((1, D_MODEL), lambda i, j: (0, 0)),
            pl.BlockSpec((D_MODEL, PROJ_TN), lambda i, j: (0, j)),
        ],
        out_specs=[
            pl.BlockSpec((tm, PROJ_TN), lambda i, j: (i, j)),
            pl.BlockSpec((SUBLANE, PROJ_TN), lambda i, j: (i, j)),
        ],
        out_shape=[
            jax.ShapeDtypeStruct((T, D_PROJ), out_dtype),
            jax.ShapeDtypeStruct((T // tm * SUBLANE, D_PROJ), F32),
        ],
        scratch_shapes=[pltpu.VMEM((tm, D_MODEL), BF16)],
        compiler_params=_cparams(("parallel", "arbitrary")),
        name="inproj",
    )(x2d, g, w)


def _gla_levels(C):
    out, s = [], C // 2
    while s >= 1:
        out.append(s)
        s //= 2
    return out


def _gla_select_matrix(C):
    r = np.arange(C)
    tril = (r[:, None] >= r[None, :])
    blocks = [tril]
    for s in _gla_levels(C):
        boundary = (r // (2 * s)) * (2 * s) + s - 1
        blocks.append(tril[boundary])
    sel = np.concatenate(blocks, axis=0)
    return jnp.asarray(np.concatenate([sel, sel, sel], axis=1), dtype=BF16)


def _gla_kernel(q_ref, k_ref, v_ref, z_ref, gd_ref, wg_ref, bg_ref, ng_ref, s0_ref, gsel_ref,
                y_ref, s_ref, st_ref, *, C):
    li = pl.program_id(2)
    HG = GLA_HEADS_PER_STEP

    @pl.when(li == 0)
    def _():
        for h in range(HG):
            st_ref[h] = s0_ref[0, h].T

    row = _iota2((C, C), 0)
    col = _iota2((C, C), 1)
    eye = row == col
    masks = [((row // (2 * s)) == (col // (2 * s))) & ((row % (2 * s)) >= s) & ((col % (2 * s)) < s)
             for s in _gla_levels(C)]
    gsel = gsel_ref[...]
    TL = q_ref.shape[1]
    probs = [(h, slice(c * C, (c + 1) * C)) for h in range(HG) for c in range(TL // C)]
    kcols = lambda h: slice(h * GLA_HK, (h + 1) * GLA_HK)
    vcols = lambda h: slice(h * GLA_HV, (h + 1) * GLA_HV)

    def stage_decay(prob):
        h, sl = prob
        x = _dot(gd_ref[0, sl, :].astype(BF16), wg_ref[:, kcols(h)]) + bg_ref[:, kcols(h)]
        la = (jnp.minimum(x, 0.0) - jnp.log1p(jnp.exp(-jnp.abs(x)))) * (1.0 / GLA_TAU)
        return dict(h=h, sl=sl, d_all=_cumsum_rows(gsel, la))

    def stage_scores(p):
        h, sl = p['h'], p['sl']
        q = q_ref[0, sl, kcols(h)].astype(F32) * (GLA_HK ** -0.5)
        k = k_ref[0, sl, kcols(h)].astype(F32)
        d_all = p['d_all']
        b = d_all[:C]
        bend = b[C - 1:C, :]
        a = jnp.where(eye, jnp.sum(q * k, axis=-1, keepdims=True), 0.0)
        for lvl, mask in enumerate(masks):
            d = b - d_all[(lvl + 1) * C:(lvl + 2) * C]
            qs = q * jnp.exp(jnp.minimum(d, 0.0))
            ks = k * jnp.exp(jnp.minimum(-d, 0.0))
            a = jnp.where(mask, _dot_nt(qs.astype(BF16), ks.astype(BF16)), a)
        return dict(h=h, sl=sl, a=a.astype(BF16), qd=(q * jnp.exp(b)).astype(BF16),
                    kd=(k * jnp.exp(bend - b)).astype(BF16), wend=jnp.exp(bend))

    def stage_values(p):
        v = v_ref[0, p['sl'], vcols(p['h'])].astype(BF16)
        p.update(o=_dot(p['a'], v), upd=_dot_tn(v, p['kd']))
        return p

    ps = [stage_decay(prob) for prob in probs]
    ps = [stage_scores(p) for p in ps]
    ps = [stage_values(p) for p in ps]

    sts = [st_ref[h] for h in range(HG)]
    for p in ps:
        h, sl = p['h'], p['sl']
        o = p['o'] + _dot_nt(p['qd'], sts[h].astype(BF16))
        sts[h] = sts[h] * p['wend'] + p['upd']
        o = o * lax.rsqrt(jnp.mean(o * o, axis=-1, keepdims=True) + NORM_EPS) * ng_ref[:, vcols(h)]
        y_ref[0, sl, vcols(h)] = (o * _silu(z_ref[0, sl, vcols(h)].astype(F32))).astype(BF16)
    for h in range(HG):
        st_ref[h] = sts[h]

    @pl.when(li == pl.num_programs(2) - 1)
    def _():
        for h in range(HG):
            s_ref[0, h] = sts[h].T


def _gla(proj, wg_pad, bg, ng, s0, C):
    B, L, _ = proj.shape
    TL = min(L, MIXER_TL)
    HG = GLA_HEADS_PER_STEP
    KW, VW = HG * GLA_HK, HG * GLA_HV
    qb, kb = COL_Q // KW, COL_K // KW
    vb, zb = COL_V // VW, COL_ZGLA // VW
    gb = COL_GD // LANE
    gsel = _gla_select_matrix(C)
    st_spec = pl.BlockSpec((1, HG, GLA_HK, GLA_HV), lambda b, h, l: (b, h, 0, 0))
    return pl.pallas_call(
        functools.partial(_gla_kernel, C=C),
        grid=(B, GLA_HEADS // HG, L // TL),
        in_specs=[
            pl.BlockSpec((1, TL, KW), lambda b, h, l: (b, l, qb + h)),
            pl.BlockSpec((1, TL, KW), lambda b, h, l: (b, l, kb + h)),
            pl.BlockSpec((1, TL, VW), lambda b, h, l: (b, l, vb + h)),
            pl.BlockSpec((1, TL, VW), lambda b, h, l: (b, l, zb + h)),
            pl.BlockSpec((1, TL, LANE), lambda b, h, l: (b, l, gb)),
            pl.BlockSpec((LANE, KW), lambda b, h, l: (0, h)),
            pl.BlockSpec((1, KW), lambda b, h, l: (0, h)),
            pl.BlockSpec((1, VW), lambda b, h, l: (0, h)),
            st_spec,
            pl.BlockSpec(gsel.shape, lambda b, h, l: (0, 0)),
        ],
        out_specs=[
            pl.BlockSpec((1, TL, VW), lambda b, h, l: (b, l, h)),
            st_spec,
        ],
        out_shape=[
            jax.ShapeDtypeStruct((B, L, GLA_DV), BF16),
            jax.ShapeDtypeStruct((B, GLA_HEADS, GLA_HK, GLA_HV), F32),
        ],
        scratch_shapes=[pltpu.VMEM((HG, GLA_HV, GLA_HK), F32)],
        compiler_params=_cparams(("parallel", "parallel", "arbitrary")),
        name="gla",
    )(proj, proj, proj, proj, proj, wg_pad, bg, ng, s0, gsel)


def _lru_kernel(x_ref, z_ref, c0_ref, h0_ref, cw_ref, cb_ref, wax_ref, ba_ref, bx_ref, lam_ref,
                y_ref, hl_ref, xpad_ref, a_ref, u_ref, hs_ref, h_ref):
    li = pl.program_id(1)
    TL = x_ref.shape[1]
    PAD = SUBLANE

    @pl.when(li == 0)
    def _():
        xpad_ref[0:PAD, :] = c0_ref[0]
        h_ref[...] = h0_ref[0]

    xpad_ref[PAD:PAD + TL, :] = x_ref[0].astype(F32)
    xc = cb_ref[...]
    for t in range(CONV_WIDTH):
        off = PAD - (CONV_WIDTH - 1) + t
        xc = xc + xpad_ref[off:off + TL, :] * cw_ref[t:t + 1, :]
    xc_b = xc.astype(BF16)
    rs, xs = [], []
    for n in range(LRU_BLOCKS):
        g = _dot(xc_b[:, n * LRU_BS:(n + 1) * LRU_BS], wax_ref[n])
        rs.append(g[:, :LRU_BS])
        xs.append(g[:, LRU_BS:])
    r = _sigmoid(jnp.concatenate(rs, axis=1) + ba_ref[...])
    i = _sigmoid(jnp.concatenate(xs, axis=1) + bx_ref[...])
    log_a = (-LRU_C) * r * _softplus(-lam_ref[...])
    a_ref[...] = jnp.exp(log_a)
    th = jnp.tanh(log_a)
    u_ref[...] = jnp.sqrt(-2.0 * th / (1.0 - th)) * (i * xc)

    row8 = _iota2((SUBLANE, LRU_WIDTH), 0)

    def body(g, h):
        base = pl.multiple_of(g * SUBLANE, SUBLANE)
        a = a_ref[pl.ds(base, SUBLANE), :]
        u = u_ref[pl.ds(base, SUBLANE), :]
        s = 1
        while s < SUBLANE:
            keep = row8 >= s
            u = u + a * jnp.where(keep, pltpu.roll(u, s, axis=0), 0.0)
            a = a * jnp.where(keep, pltpu.roll(a, s, axis=0), 1.0)
            s *= 2
        hs = a * h + u
        hs_ref[pl.ds(base, SUBLANE), :] = hs
        return hs[SUBLANE - 1:SUBLANE, :]

    h = lax.fori_loop(0, TL // SUBLANE, body, h_ref[...], unroll=2)
    h_ref[...] = h
    hl_ref[0] = h
    y_ref[0] = (hs_ref[...] * _silu(z_ref[0].astype(F32))).astype(BF16)
    xpad_ref[0:PAD, :] = xpad_ref[TL:TL + PAD, :]


def _lru(proj, conv0_pad, h0, cw, cb, wax, ba, bx, lam):
    B, L, _ = proj.shape
    TL = min(L, MIXER_TL)
    W = LRU_WIDTH
    xb, zb = COL_XL // W, COL_ZLRU // W
    vec = pl.BlockSpec((1, W), lambda b, l: (0, 0))
    return pl.pallas_call(
        _lru_kernel,
        grid=(B, L // TL),
        in_specs=[
            pl.BlockSpec((1, TL, W), lambda b, l: (b, l, xb)),
            pl.BlockSpec((1, TL, W), lambda b, l: (b, l, zb)),
            pl.BlockSpec((1, SUBLANE, W), lambda b, l: (b, 0, 0)),
            pl.BlockSpec((1, 1, W), lambda b, l: (b, 0, 0)),
            pl.BlockSpec((CONV_WIDTH, W), lambda b, l: (0, 0)),
            vec,
            pl.BlockSpec((LRU_BLOCKS, LRU_BS, 2 * LRU_BS), lambda b, l: (0, 0, 0)),
            vec, vec, vec,
        ],
        out_specs=[
            pl.BlockSpec((1, TL, W), lambda b, l: (b, l, 0)),
            pl.BlockSpec((1, 1, W), lambda b, l: (b, 0, 0)),
        ],
        out_shape=[
            jax.ShapeDtypeStruct((B, L, W), BF16),
            jax.ShapeDtypeStruct((B, 1, W), F32),
        ],
        scratch_shapes=[
            pltpu.VMEM((TL + SUBLANE, W), F32),
            pltpu.VMEM((TL, W), F32),
            pltpu.VMEM((TL, W), F32),
            pltpu.VMEM((TL, W), F32),
            pltpu.VMEM((1, W), F32),
        ],
        compiler_params=_cparams(("parallel", "arbitrary")),
        name="rglru",
    )(proj, proj, conv0_pad, h0, cw, cb, wax, ba, bx, lam)


def _headsum(x, ones_bd):
    xb = x.astype(BF16)
    outs = [_dot(xb[:, g * RWKV_GW:(g + 1) * RWKV_GW], ones_bd) for g in range(RWKV_NGROUPS)]
    return jnp.concatenate(outs, axis=1)


def _rwkv_local(loads, C, masks):
    tril_b, strict_rb, incl_rb, eye_rb, bd_sq, bd_cv, bd_state = masks

    def tile_rows(x, n):
        return jnp.concatenate([x] * n, axis=0)

    def blockdiag_cv(x):
        return tile_rows(x.astype(BF16), RWKV_GROUP) * bd_cv

    def blockdiag_sq(x):
        return tile_rows(x.astype(BF16), RWKV_GROUP) * bd_sq

    def stage_decay(load):
        r, k, v, kk, a, lw = load()
        lc = _cumsum_rows(tril_b, lw)
        wend = jnp.exp(lc[C - 1:C, :])
        einv = jnp.exp(-lc)
        bt = kk * jnp.exp(lc - lw)
        rt = r * jnp.exp(lc)
        at = -(kk * a) * einv
        kt = k * einv
        return dict(v=v, wend=wend, bt=bt, rt=rt, at=at, kt=kt)

    def stage_scores(p):
        br = jnp.concatenate([p['bt'], p['rt']], axis=0).astype(BF16)
        pa = _dot_nt(br, blockdiag_cv(p['at']))
        pk = _dot_nt(br, blockdiag_cv(p['kt']))
        a_ba = jnp.where(strict_rb, pa[:C], 0.0)
        p.update(a_ra=jnp.where(incl_rb, pa[C:], 0.0).astype(BF16),
                 a_bk=jnp.where(strict_rb, pk[:C], 0.0).astype(BF16),
                 a_rk=jnp.where(incl_rb, pk[C:], 0.0).astype(BF16),
                 x=a_ba, xb=blockdiag_sq(a_ba), t=eye_rb + a_ba)
        return p

    def stage_square(p):
        p['x'] = _dot(p['x'].astype(BF16), p['xb'])
        p['xb'] = blockdiag_sq(p['x'])
        return p

    def stage_double(p, last):
        if last:
            p['t'] = p['t'] + _dot(p['t'].astype(BF16), p['xb'])
            return p
        xt = _dot(jnp.concatenate([p['x'], p['t']], axis=0).astype(BF16), p['xb'])
        p['x'] = xt[:C]
        p['t'] = p['t'] + xt[C:]
        p['xb'] = blockdiag_sq(p['x'])
        return p

    def stage_apply(p):
        tb = p['t'].astype(BF16)
        v_bd = blockdiag_cv(p['v'])
        xv = _dot(jnp.concatenate([p['a_bk'], p['a_rk']], axis=0), v_bd)
        p.update(yv=xv[C:], bhat=_dot(tb, blockdiag_cv(p['bt'])),
                 uloc=_dot(tb, blockdiag_cv(xv[:C])))
        return p

    def stage_out(p):
        rhat = p['rt'] + _dot(p['a_ra'], blockdiag_cv(p['bhat']))
        yloc = _dot(p['a_ra'], blockdiag_cv(p['uloc'])) + p['yv']
        atw = (p['at'] * p['wend']).astype(BF16)
        ktw = (p['kt'] * p['wend']).astype(BF16)
        ghat = jnp.where(bd_state, _dot_tn(p['bhat'].astype(BF16), atw), 0.0)
        h = jnp.where(bd_state,
                      _dot_tn(jnp.concatenate([p['uloc'], p['v']], axis=0).astype(BF16),
                              jnp.concatenate([atw, ktw], axis=0)), 0.0)
        return rhat, yloc, ghat, h, p['wend']

    ps = [stage_decay(ld) for ld in loads]
    ps = [stage_scores(p) for p in ps]
    ps = [stage_square(p) for p in ps]
    n = 4
    while n <= C:
        ps = [stage_double(p, last=(n == C)) for p in ps]
        n *= 2
    ps = [stage_apply(p) for p in ps]
    return [stage_out(p) for p in ps]


def _rwkv_masks(C):
    G = RWKV_GROUP
    tril = (_iota2((C, 3 * C), 0) >= (_iota2((C, 3 * C), 1) % C)).astype(BF16)
    r_rb = _iota2((C, G * C), 0)
    c_rb = _iota2((C, G * C), 1) % C
    strict_rb = r_rb > c_rb
    incl_rb = r_rb >= c_rb
    eye_rb = (r_rb == c_rb).astype(F32)
    bd_sq = ((_iota2((G * C, G * C), 0) // C) == (_iota2((G * C, G * C), 1) // C)).astype(BF16)
    bd_cv = ((_iota2((G * C, RWKV_GW), 0) // C) == (_iota2((G * C, RWKV_GW), 1) // RWKV_HD)).astype(BF16)
    bd_state = (_iota2((RWKV_GW, RWKV_GW), 0) // RWKV_HD) == (_iota2((RWKV_GW, RWKV_GW), 1) // RWKV_HD)
    return tril, strict_rb, incl_rb, eye_rb, bd_sq, bd_cv, bd_state


def _rwkv_kernel(r_ref, k_ref, v_ref, wa_ref, z_ref, sh_rkv_ref, sh_wa_ref, s0_ref,
                 mu_rkv_ref, mu_wa_ref, w0_ref, wup_ref, a0_ref, aup_ref, kk_ref, ka_ref, rk_ref,
                 gnw_ref, gnb_ref,
                 y_ref, s_ref,
                 xpad_ref, wapad_ref, st_ref, rs_ref, ks_ref, vs_ref, kks_ref, as_ref, lws_ref, ys_ref,
                 rhat_ref, ghat_ref, h_ref, wend_ref, *, C):
    li = pl.program_id(1)
    TL = r_ref.shape[1]
    PAD = SUBLANE
    W = RWKV_WIDTH

    @pl.when(li == 0)
    def _():
        for n in range(3):
            xpad_ref[n, PAD - 1:PAD, :] = sh_rkv_ref[0, :, n * W:(n + 1) * W]
        wapad_ref[PAD - 1:PAD, :] = sh_wa_ref[0]
        zero = jnp.zeros((RWKV_HD, RWKV_HD), F32)
        for g in range(RWKV_NGROUPS):
            st_ref[g] = jnp.concatenate(
                [jnp.concatenate([s0_ref[0, g * RWKV_GROUP + i] if i == j else zero
                                  for i in range(RWKV_GROUP)], axis=1)
                 for j in range(RWKV_GROUP)], axis=0)

    mixed = []
    for n, ref in enumerate((r_ref, k_ref, v_ref)):
        x = ref[0].astype(F32)
        xpad_ref[n, PAD:PAD + TL, :] = x
        prev = xpad_ref[n, PAD - 1:PAD - 1 + TL, :]
        mixed.append(x + (prev - x) * mu_rkv_ref[:, n * W:(n + 1) * W])
        xpad_ref[n, PAD - 1:PAD, :] = xpad_ref[n, PAD + TL - 1:PAD + TL, :]
    r, k, v = mixed
    xwa = wa_ref[0].astype(F32)
    wapad_ref[PAD:PAD + TL, :] = xwa
    prev = wapad_ref[PAD - 1:PAD - 1 + TL, :]
    wapad_ref[PAD - 1:PAD, :] = wapad_ref[PAD + TL - 1:PAD + TL, :]
    xwa = xwa + (prev - xwa) * mu_wa_ref[...]

    w = w0_ref[...] + _dot(jnp.tanh(xwa).astype(BF16), wup_ref[...])
    lw = (-RWKV_DECAY_SCALE) * _sigmoid(w)
    a = _sigmoid(a0_ref[...] + _dot(xwa.astype(BF16), aup_ref[...]))

    ones_bd = ((_iota2((RWKV_GW, RWKV_GW), 0) // RWKV_HD)
               == (_iota2((RWKV_GW, RWKV_GW), 1) // RWKV_HD)).astype(BF16)
    kk = k * kk_ref[...]
    kk = kk * lax.rsqrt(_headsum(kk * kk, ones_bd) + 1e-12)
    k = k * (1.0 + (a - 1.0) * ka_ref[...])
    rs_ref[...] = r
    ks_ref[...] = k
    vs_ref[...] = v
    kks_ref[...] = kk
    as_ref[...] = a
    lws_ref[...] = lw

    masks = _rwkv_masks(C)

    NCH = TL // C
    NB = min(NCH, RWKV_LOCAL_CHUNKS)

    def loader(rows, cols):
        return lambda: tuple(ref[rows, cols]
                             for ref in (rs_ref, ks_ref, vs_ref, kks_ref, as_ref, lws_ref))

    for cb in range(NCH // NB):
        probs = []
        for c in range(cb * NB, (cb + 1) * NB):
            rows = slice(c * C, (c + 1) * C)
            for g in range(RWKV_NGROUPS):
                probs.append((c, rows, g, slice(g * RWKV_GW, (g + 1) * RWKV_GW)))
        outs = _rwkv_local([loader(rows, cols) for _, rows, _, cols in probs], C, masks)
        for (c, rows, g, cols), (rhat, yloc, ghat, h, wend) in zip(probs, outs):
            rhat_ref[rows, cols] = rhat.astype(BF16)
            ys_ref[rows, cols] = yloc
            ghat_ref[c, g] = ghat.astype(BF16)
            h_ref[c, g] = h
            wend_ref[c, :, cols] = wend

    states = [st_ref[g] for g in range(RWKV_NGROUPS)]
    for c in range(NCH):
        rows = slice(c * C, (c + 1) * C)
        for g in range(RWKV_NGROUPS):
            cols = slice(g * RWKV_GW, (g + 1) * RWKV_GW)
            sb = states[g].astype(BF16)
            ys_ref[rows, cols] = ys_ref[rows, cols] + _dot_nt(rhat_ref[rows, cols], sb)
            states[g] = states[g] * wend_ref[c, :, cols] + _dot(sb, ghat_ref[c, g]) + h_ref[c, g]
    for g in range(RWKV_NGROUPS):
        st_ref[g] = states[g]

    y = ys_ref[...]
    r = rs_ref[...]
    k = ks_ref[...]
    v = vs_ref[...]
    inv_hd = 1.0 / RWKV_HD
    mean = _headsum(y, ones_bd) * inv_hd
    d = y - mean
    var = _headsum(d * d, ones_bd) * inv_hd
    yn = d * lax.rsqrt(var + RWKV_GN_EPS) * gnw_ref[...] + gnb_ref[...]
    yn = yn + _headsum(r * k * rk_ref[...], ones_bd) * v
    y_ref[0] = (yn * _silu(z_ref[0].astype(F32))).astype(BF16)

    @pl.when(li == pl.num_programs(1) - 1)
    def _():
        for g in range(RWKV_NGROUPS):
            for j in range(RWKV_GROUP):
                d = slice(j * RWKV_HD, (j + 1) * RWKV_HD)
                s_ref[0, g * RWKV_GROUP + j] = states[g][d, d]


def _rwkv(proj, sh_rkv, sh_wa, s0, mu_rkv, mu_wa, w0, wup_pad, a0, aup_pad, k_k, k_a, r_k, gn_w, gn_b, C):
    B, L, _ = proj.shape
    TL = min(L, MIXER_TL)
    W = RWKV_WIDTH
    rb, kb, vb, zb = COL_R // W, COL_KRW // W, COL_VRW // W, COL_ZRW // W
    wab = COL_WA // LANE
    vec = pl.BlockSpec((1, W), lambda b, l: (0, 0))
    mat = pl.BlockSpec((LANE, W), lambda b, l: (0, 0))
    tok = lambda cb: pl.BlockSpec((1, TL, W), lambda b, l: (b, l, cb))
    st_spec = pl.BlockSpec((1, RWKV_HEADS, RWKV_HD, RWKV_HD), lambda b, l: (b, 0, 0, 0))
    return pl.pallas_call(
        functools.partial(_rwkv_kernel, C=C),
        grid=(B, L // TL),
        in_specs=[
            tok(rb), tok(kb), tok(vb),
            pl.BlockSpec((1, TL, LANE), lambda b, l: (b, l, wab)),
            tok(zb),
            pl.BlockSpec((1, 1, 3 * W), lambda b, l: (b, 0, 0)),
            pl.BlockSpec((1, 1, LANE), lambda b, l: (b, 0, 0)),
            st_spec,
            pl.BlockSpec((1, 3 * W), lambda b, l: (0, 0)),
            pl.BlockSpec((1, LANE), lambda b, l: (0, 0)),
            vec, mat, vec, mat, vec, vec, vec, vec, vec,
        ],
        out_specs=[
            pl.BlockSpec((1, TL, W), lambda b, l: (b, l, 0)),
            st_spec,
        ],
        out_shape=[
            jax.ShapeDtypeStruct((B, L, W), BF16),
            jax.ShapeDtypeStruct((B, RWKV_HEADS, RWKV_HD, RWKV_HD), F32),
        ],
        scratch_shapes=[
            pltpu.VMEM((3, TL + SUBLANE, W), F32),
            pltpu.VMEM((TL + SUBLANE, LANE), F32),
            pltpu.VMEM((RWKV_NGROUPS, RWKV_GW, RWKV_GW), F32),
        ] + [pltpu.VMEM((TL, W), F32)] * 7 + [
            pltpu.VMEM((TL, W), BF16),
            pltpu.VMEM((TL // C, RWKV_NGROUPS, RWKV_GW, RWKV_GW), BF16),
            pltpu.VMEM((TL // C, RWKV_NGROUPS, RWKV_GW, RWKV_GW), F32),
            pltpu.VMEM((TL // C, 1, W), F32),
        ],
        compiler_params=_cparams(("parallel", "arbitrary")),
        name="rwkv7",
    )(proj, proj, proj, proj, proj, sh_rkv, sh_wa, s0,
      mu_rkv, mu_wa, w0, wup_pad, a0, aup_pad, k_k, k_a, r_k, gn_w, gn_b)


def _outproj_kernel(x_ref, yg_ref, yl_ref, yr_ref, ga_ref, gb_ref, gc_ref,
                    wg_ref, wl_ref, wr_ref, wo_ref, fg_ref, o_ref, *, final):
    merged = (_sigmoid(ga_ref[...].astype(F32)) * _dot(yg_ref[...], wg_ref[...])
              + _sigmoid(gb_ref[...].astype(F32)) * _dot(yl_ref[...], wl_ref[...])
              + _sigmoid(gc_ref[...].astype(F32)) * _dot(yr_ref[...], wr_ref[...]))
    x = x_ref[...] + _dot(merged.astype(BF16), wo_ref[...])
    if final:
        x = x * lax.rsqrt(jnp.mean(x * x, axis=-1, keepdims=True) + NORM_EPS) * fg_ref[...]
    o_ref[...] = x


def _outproj(x2d, yg, yl, yr, proj2d, wg, wl, wr, wo, fg, final):
    T = x2d.shape[0]
    tm = min(T, OUTPROJ_TM)
    D = D_MODEL
    gb = COL_GATES // D
    tok = lambda cb: pl.BlockSpec((tm, D), lambda i: (i, cb))
    wspec = pl.BlockSpec((D, D), lambda i: (0, 0))
    return pl.pallas_call(
        functools.partial(_outproj_kernel, final=final),
        grid=(T // tm,),
        in_specs=[tok(0), tok(0), tok(0), tok(0), tok(gb), tok(gb + 1), tok(gb + 2),
                  wspec, wspec, wspec, wspec, pl.BlockSpec((1, D), lambda i: (0, 0))],
        out_specs=tok(0),
        out_shape=jax.ShapeDtypeStruct((T, D), F32),
        compiler_params=_cparams(("parallel",)),
        name="outproj",
    )(x2d, yg, yl, yr, proj2d, proj2d, proj2d, wg, wl, wr, wo, fg)


def _prep_layer(P, l):
    w_in = P['w_in'][l]
    o_gd = GLA_DK + GLA_DK + GLA_DV
    o_zg = o_gd + GLA_RANK
    o_wa = o_zg + GLA_DV + 2 * LRU_WIDTH + 3 * RWKV_WIDTH
    o_zr = o_wa + DECAY_RANK + AAA_RANK
    w = jnp.concatenate([
        w_in[:, :o_gd].astype(BF16),
        w_in[:, o_zg:o_wa].astype(BF16),
        w_in[:, o_zr:].astype(BF16),
        w_in[:, o_gd:o_zg].astype(BF16),
        jnp.zeros((D_MODEL, LANE - GLA_RANK), BF16),
        w_in[:, o_wa:o_zr].astype(BF16),
    ], axis=1)
    zpad = jnp.zeros((LANE - GLA_RANK, GLA_DK), F32)
    z64 = jnp.zeros((DECAY_RANK, RWKV_WIDTH), F32)
    mu = P['rwkv_mu'][l]
    row = lambda a: a.reshape(1, -1)
    return dict(
        norm_g=row(P['norm_g'][l]), w_in=w,
        wg_pad=jnp.concatenate([P['gla_w_gup'][l], zpad], axis=0).astype(BF16),
        bg=row(P['gla_b_g'][l]), ng=row(P['gla_norm_g'][l]),
        cw=P['lru_conv_w'][l], cb=row(P['lru_conv_b'][l]),
        wax=jnp.concatenate([P['lru_w_a'][l], P['lru_w_x'][l]], axis=-1).astype(BF16),
        ba=row(P['lru_b_a'][l]), bx=row(P['lru_b_x'][l]), lam=row(P['lru_lambda'][l]),
        mu_rkv=row(mu[:3 * RWKV_WIDTH]), mu_wa=row(mu[3 * RWKV_WIDTH:]),
        w0=row(P['rwkv_w0'][l]),
        wup_pad=jnp.concatenate([P['rwkv_w_up'][l], z64], axis=0).astype(BF16),
        a0=row(P['rwkv_a0'][l]),
        aup_pad=jnp.concatenate([z64, P['rwkv_a_up'][l]], axis=0).astype(BF16),
        k_k=row(P['rwkv_k_k'][l]), k_a=row(P['rwkv_k_a'][l]), r_k=row(P['rwkv_r_k'][l]),
        gn_w=row(P['rwkv_gn_w'][l]), gn_b=row(P['rwkv_gn_b'][l]),
        wpg=P['w_proj_gla'][l].astype(BF16), wpl=P['w_proj_lru'][l].astype(BF16),
        wpr=P['w_proj_rwkv'][l].astype(BF16), wo=P['w_out'][l].astype(BF16),
    )


def _to_kernel_states(gla, lru_h, lru_conv, rwkv, shift):
    Bn = gla.shape[1]
    conv_pad = jnp.zeros((DEPTH, Bn, SUBLANE - (CONV_WIDTH - 1), LRU_WIDTH), lru_conv.dtype)
    return (gla, lru_h[:, :, None, :], jnp.concatenate([conv_pad, lru_conv], axis=2), rwkv,
            shift[:, :, None, :3 * RWKV_WIDTH], shift[:, :, None, 3 * RWKV_WIDTH:])


def _zero_kernel_states(Bn, dt):
    z = lambda *shape: jnp.zeros((DEPTH, Bn) + shape, dt)
    return (z(GLA_HEADS, GLA_HK, GLA_HV), z(1, LRU_WIDTH), z(SUBLANE, LRU_WIDTH),
            z(RWKV_HEADS, RWKV_HD, RWKV_HD), z(1, 3 * RWKV_WIDTH), z(1, LANE))


def _run_trunk(x, states, layers, final_g, C):
    gla0, lru_h0, conv0_pad, rwkv0, sh_rkv, sh_wa = states
    B, L, D = x.shape
    T = B * L
    x2d = x.reshape(T, D)
    n_gla, n_h, n_conv, n_rw, n_shift = [], [], [], [], []
    for l in range(DEPTH):
        p = layers[l]
        tiles_per_seq = L // INPROJ_TM if L % INPROJ_TM == 0 else 0
        proj2d, tails = _inproj(x2d, p['norm_g'], p['w_in'], BF16 if tiles_per_seq else F32)
        proj = proj2d.reshape(B, L, D_PROJ)
        if tiles_per_seq:
            tail = tails.reshape(B, tiles_per_seq, SUBLANE, D_PROJ)[:, -1]
        else:
            tail = proj[:, L - SUBLANE:, :]

        yg, s_gla = _gla(proj, p['wg_pad'], p['bg'], p['ng'], gla0[l], C)
        yl, h_last = _lru(proj, conv0_pad[l], lru_h0[l], p['cw'], p['cb'], p['wax'],
                          p['ba'], p['bx'], p['lam'])
        yr, s_rw = _rwkv(proj, sh_rkv[l], sh_wa[l], rwkv0[l],
                         p['mu_rkv'], p['mu_wa'], p['w0'], p['wup_pad'], p['a0'], p['aup_pad'],
                         p['k_k'], p['k_a'], p['r_k'], p['gn_w'], p['gn_b'], C)

        x2d = _outproj(x2d, yg.reshape(T, D), yl.reshape(T, D), yr.reshape(T, D), proj2d,
                       p['wpg'], p['wpl'], p['wpr'], p['wo'], final_g, final=(l == DEPTH - 1))

        n_gla.append(s_gla)
        n_h.append(h_last)
        n_conv.append(tail[:, SUBLANE - (CONV_WIDTH - 1):, COL_XL:COL_XL + LRU_WIDTH])
        n_rw.append(s_rw)
        n_shift.append(jnp.concatenate(
            [tail[:, -1, COL_R:COL_R + 3 * RWKV_WIDTH], tail[:, -1, COL_WA:COL_WA + 2 * DECAY_RANK]],
            axis=-1))
    return (x2d.reshape(B, L, D), jnp.stack(n_gla), jnp.stack(n_h)[:, :, 0, :],
            jnp.stack(n_conv), jnp.stack(n_rw), jnp.stack(n_shift))


def kernel(x_prompt, x_sample, state_gla, state_lru_h, state_lru_conv, state_rwkv, state_rwkv_shift,
           norm_g, w_in, gla_w_gup, gla_b_g, gla_norm_g,
           lru_conv_w, lru_conv_b, lru_w_a, lru_b_a, lru_w_x, lru_b_x, lru_lambda,
           rwkv_mu, rwkv_w0, rwkv_w_up, rwkv_a0, rwkv_a_up, rwkv_k_k, rwkv_k_a, rwkv_r_k,
           rwkv_gn_w, rwkv_gn_b, w_proj_gla, w_proj_lru, w_proj_rwkv, w_out, final_norm_g):
    P = dict(norm_g=norm_g, w_in=w_in, gla_w_gup=gla_w_gup, gla_b_g=gla_b_g, gla_norm_g=gla_norm_g,
             lru_conv_w=lru_conv_w, lru_conv_b=lru_conv_b, lru_w_a=lru_w_a, lru_b_a=lru_b_a,
             lru_w_x=lru_w_x, lru_b_x=lru_b_x, lru_lambda=lru_lambda,
             rwkv_mu=rwkv_mu, rwkv_w0=rwkv_w0, rwkv_w_up=rwkv_w_up, rwkv_a0=rwkv_a0,
             rwkv_a_up=rwkv_a_up, rwkv_k_k=rwkv_k_k, rwkv_k_a=rwkv_k_a, rwkv_r_k=rwkv_r_k,
             rwkv_gn_w=rwkv_gn_w, rwkv_gn_b=rwkv_gn_b, w_proj_gla=w_proj_gla, w_proj_lru=w_proj_lru,
             w_proj_rwkv=w_proj_rwkv, w_out=w_out)
    layers = [_prep_layer(P, l) for l in range(DEPTH)]
    final_g = final_norm_g.reshape(1, -1)
    out_p = _run_trunk(x_prompt, _zero_kernel_states(x_prompt.shape[0], x_prompt.dtype),
                       layers, final_g, C=64)
    Ls = x_sample.shape[1]
    out_s = _run_trunk(x_sample, _to_kernel_states(state_gla, state_lru_h, state_lru_conv,
                                                   state_rwkv, state_rwkv_shift),
                       layers, final_g, C=64 if Ls % 64 == 0 else Ls)
    return (out_p[0], out_s[0]) + tuple(out_p[1:]) + tuple(out_s[1:])
```

```python
import functools

import numpy as np
import jax
import jax.numpy as jnp
from jax import lax
from jax.experimental import pallas as pl
from jax.experimental.pallas import tpu as pltpu

F32 = jnp.float32
BF16 = jnp.bfloat16

D_MODEL = 1024
DEPTH = 4
NORM_EPS = 1e-6
GLA_HEADS = 4
GLA_HK = 128
GLA_HV = 256
GLA_DK = GLA_HEADS * GLA_HK
GLA_DV = GLA_HEADS * GLA_HV
GLA_RANK = 16
GLA_TAU = 16.0
GLA_HEADS_PER_STEP = 4
LRU_WIDTH = 1024
LRU_BLOCKS = 8
LRU_BS = 128
CONV_WIDTH = 4
LRU_C = 8.0
RWKV_WIDTH = 1024
RWKV_HD = 64
RWKV_HEADS = 16
RWKV_GROUP = 2
RWKV_GW = RWKV_GROUP * RWKV_HD
RWKV_NGROUPS = RWKV_HEADS // RWKV_GROUP
RWKV_LOCAL_CHUNKS = 2
DECAY_RANK = 64
AAA_RANK = 64
RWKV_GN_EPS = 64e-5
RWKV_DECAY_SCALE = 0.6065306597126334
SHIFT_WIDTH = 3 * RWKV_WIDTH + DECAY_RANK + AAA_RANK

LANE = 128
SUBLANE = 8

COL_Q = 0
COL_K = 512
COL_V = 1024
COL_ZGLA = 2048
COL_XL = 3072
COL_ZLRU = 4096
COL_R = 5120
COL_KRW = 6144
COL_VRW = 7168
COL_ZRW = 8192
COL_GATES = 9216
COL_GD = 12288
COL_WA = 12416
D_PROJ = 12544
PROJ_TN = 1792
INPROJ_TM = 1024
OUTPROJ_TM = 256
MIXER_TL = 256

VMEM_LIMIT = 56 * 1024 * 1024


def _cparams(sem):
    return pltpu.CompilerParams(dimension_semantics=sem, vmem_limit_bytes=VMEM_LIMIT)


def _dot(a, b):
    return jnp.dot(a, b, preferred_element_type=F32)


def _dot_nt(a, b):
    return lax.dot_general(a, b, (((1,), (1,)), ((), ())), preferred_element_type=F32)


def _dot_tn(a, b):
    return lax.dot_general(a, b, (((0,), (0,)), ((), ())), preferred_element_type=F32)


def _sigmoid(x):
    return 1.0 / (1.0 + jnp.exp(-x))


def _silu(x):
    return x * _sigmoid(x)


def _softplus(x):
    return jnp.maximum(x, 0.0) + jnp.log1p(jnp.exp(-jnp.abs(x)))


def _iota2(shape, dim):
    return lax.broadcasted_iota(jnp.int32, shape, dim)


def _cumsum_rows(sel3, x):
    hi = x.astype(BF16)
    r1 = x - hi.astype(F32)
    mid = r1.astype(BF16)
    lo = (r1 - mid.astype(F32)).astype(BF16)
    return _dot(sel3, jnp.concatenate([hi, mid, lo], axis=0))


def _inproj_kernel(x_ref, g_ref, w_ref, o_ref, tail_ref, xn_ref):
    @pl.when(pl.program_id(1) == 0)
    def _():
        x = x_ref[...]
        y = x * lax.rsqrt(jnp.mean(x * x, axis=-1, keepdims=True) + NORM_EPS)
        xn_ref[...] = (y * g_ref[...]).astype(BF16)

    acc = _dot(xn_ref[...], w_ref[...])
    o_ref[...] = acc.astype(o_ref.dtype)
    tail_ref[...] = acc[acc.shape[0] - SUBLANE:, :]


def _inproj(x2d, g, w, out_dtype):
    T = x2d.shape[0]
    tm = min(T, INPROJ_TM)
    return pl.pallas_call(
        _inproj_kernel,
        grid=(T // tm, D_PROJ // PROJ_TN),
        in_specs=[
            pl.BlockSpec((tm, D_MODEL), lambda i, j: (i, 0)),
            pl.BlockSpec((1, D_MODEL), lambda i, j: (0, 0)),
            pl.BlockSpec((D_MODEL, PROJ_TN), lambda i, j: (0, j)),
        ],
        out_specs=[
            pl.BlockSpec((tm, PROJ_TN), lambda i, j: (i, j)),
            pl.BlockSpec((SUBLANE, PROJ_TN), lambda i, j: (i, j)),
        ],
        out_shape=[
            jax.ShapeDtypeStruct((T, D_PROJ), out_dtype),
            jax.ShapeDtypeStruct((T // tm * SUBLANE, D_PROJ), F32),
        ],
        scratch_shapes=[pltpu.VMEM((tm, D_MODEL), BF16)],
        compiler_params=_cparams(("parallel", "arbitrary")),
        name="inproj",
    )(x2d, g, w)


def _gla_levels(C):
    out, s = [], C // 2
    while s >= 1:
        out.append(s)
        s //= 2
    return out


def _gla_select_matrix(C):
    r = np.arange(C)
    tril = (r[:, None] >= r[None, :])
    blocks = [tril]
    for s in _gla_levels(C):
        boundary = (r // (2 * s)) * (2 * s) + s - 1
        blocks.append(tril[boundary])
    sel = np.concatenate(blocks, axis=0)
    return jnp.asarray(np.concatenate([sel, sel, sel], axis=1), dtype=BF16)


def _gla_kernel(q_ref, k_ref, v_ref, z_ref, gd_ref, wg_ref, bg_ref, ng_ref, s0_ref, gsel_ref,
                y_ref, s_ref, st_ref, *, C):
    li = pl.program_id(2)
    HG = GLA_HEADS_PER_STEP

    @pl.when(li == 0)
    def _():
        for h in range(HG):
            st_ref[h] = s0_ref[0, h].T

    row = _iota2((C, C), 0)
    col = _iota2((C, C), 1)
    eye = row == col
    masks = [((row // (2 * s)) == (col // (2 * s))) & ((row % (2 * s)) >= s) & ((col % (2 * s)) < s)
             for s in _gla_levels(C)]
    gsel = gsel_ref[...]
    TL = q_ref.shape[1]
    probs = [(h, slice(c * C, (c + 1) * C)) for h in range(HG) for c in range(TL // C)]
    kcols = lambda h: slice(h * GLA_HK, (h + 1) * GLA_HK)
    vcols = lambda h: slice(h * GLA_HV, (h + 1) * GLA_HV)

    def stage_decay(prob):
        h, sl = prob
        x = _dot(gd_ref[0, sl, :].astype(BF16), wg_ref[:, kcols(h)]) + bg_ref[:, kcols(h)]
        la = (jnp.minimum(x, 0.0) - jnp.log1p(jnp.exp(-jnp.abs(x)))) * (1.0 / GLA_TAU)
        return dict(h=h, sl=sl, d_all=_cumsum_rows(gsel, la))

    def stage_scores(p):
        h, sl = p['h'], p['sl']
        q = q_ref[0, sl, kcols(h)].astype(F32) * (GLA_HK ** -0.5)
        k = k_ref[0, sl, kcols(h)].astype(F32)
        d_all = p['d_all']
        b = d_all[:C]
        bend = b[C - 1:C, :]
        a = jnp.where(eye, jnp.sum(q * k, axis=-1, keepdims=True), 0.0)
        for lvl, mask in enumerate(masks):
            d = b - d_all[(lvl + 1) * C:(lvl + 2) * C]
            qs = q * jnp.exp(jnp.minimum(d, 0.0))
            ks = k * jnp.exp(jnp.minimum(-d, 0.0))
            a = jnp.where(mask, _dot_nt(qs.astype(BF16), ks.astype(BF16)), a)
        return dict(h=h, sl=sl, a=a.astype(BF16), qd=(q * jnp.exp(b)).astype(BF16),
                    kd=(k * jnp.exp(bend - b)).astype(BF16), wend=jnp.exp(bend))

    def stage_values(p):
        v = v_ref[0, p['sl'], vcols(p['h'])].astype(BF16)
        p.update(o=_dot(p['a'], v), upd=_dot_tn(v, p['kd']))
        return p

    ps = [stage_decay(prob) for prob in probs]
    ps = [stage_scores(p) for p in ps]
    ps = [stage_values(p) for p in ps]

    sts = [st_ref[h] for h in range(HG)]
    for p in ps:
        h, sl = p['h'], p['sl']
        o = p['o'] + _dot_nt(p['qd'], sts[h].astype(BF16))
        sts[h] = sts[h] * p['wend'] + p['upd']
        o = o * lax.rsqrt(jnp.mean(o * o, axis=-1, keepdims=True) + NORM_EPS) * ng_ref[:, vcols(h)]
        y_ref[0, sl, vcols(h)] = (o * _silu(z_ref[0, sl, vcols(h)].astype(F32))).astype(BF16)
    for h in range(HG):
        st_ref[h] = sts[h]

    @pl.when(li == pl.num_programs(2) - 1)
    def _():
        for h in range(HG):
            s_ref[0, h] = sts[h].T


def _gla(proj, wg_pad, bg, ng, s0, C):
    B, L, _ = proj.shape
    TL = min(L, MIXER_TL)
    HG = GLA_HEADS_PER_STEP
    KW, VW = HG * GLA_HK, HG * GLA_HV
    qb, kb = COL_Q // KW, COL_K // KW
    vb, zb = COL_V // VW, COL_ZGLA // VW
    gb = COL_GD // LANE
    gsel = _gla_select_matrix(C)
    st_spec = pl.BlockSpec((1, HG, GLA_HK, GLA_HV), lambda b, h, l: (b, h, 0, 0))
    return pl.pallas_call(
        functools.partial(_gla_kernel, C=C),
        grid=(B, GLA_HEADS // HG, L // TL),
        in_specs=[
            pl.BlockSpec((1, TL, KW), lambda b, h, l: (b, l, qb + h)),
            pl.BlockSpec((1, TL, KW), lambda b, h, l: (b, l, kb + h)),
            pl.BlockSpec((1, TL, VW), lambda b, h, l: (b, l, vb + h)),
            pl.BlockSpec((1, TL, VW), lambda b, h, l: (b, l, zb + h)),
            pl.BlockSpec((1, TL, LANE), lambda b, h, l: (b, l, gb)),
            pl.BlockSpec((LANE, KW), lambda b, h, l: (0, h)),
            pl.BlockSpec((1, KW), lambda b, h, l: (0, h)),
            pl.BlockSpec((1, VW), lambda b, h, l: (0, h)),
            st_spec,
            pl.BlockSpec(gsel.shape, lambda b, h, l: (0, 0)),
        ],
        out_specs=[
            pl.BlockSpec((1, TL, VW), lambda b, h, l: (b, l, h)),
            st_spec,
        ],
        out_shape=[
            jax.ShapeDtypeStruct((B, L, GLA_DV), BF16),
            jax.ShapeDtypeStruct((B, GLA_HEADS, GLA_HK, GLA_HV), F32),
        ],
        scratch_shapes=[pltpu.VMEM((HG, GLA_HV, GLA_HK), F32)],
        compiler_params=_cparams(("parallel", "parallel", "arbitrary")),
        name="gla",
    )(proj, proj, proj, proj, proj, wg_pad, bg, ng, s0, gsel)


def _lru_kernel(x_ref, z_ref, c0_ref, h0_ref, cw_ref, cb_ref, wax_ref, ba_ref, bx_ref, lam_ref,
                y_ref, hl_ref, xpad_ref, a_ref, u_ref, hs_ref, h_ref):
    li = pl.program_id(1)
    TL = x_ref.shape[1]
    PAD = SUBLANE

    @pl.when(li == 0)
    def _():
        xpad_ref[0:PAD, :] = c0_ref[0]
        h_ref[...] = h0_ref[0]

    xpad_ref[PAD:PAD + TL, :] = x_ref[0].astype(F32)
    xp = xpad_ref[...]
    acc = xp * cw_ref[0:1, :]
    for t in range(1, CONV_WIDTH):
        acc = pltpu.roll(acc, 1, axis=0) + xp * cw_ref[t:t + 1, :]
    xc = acc[PAD:, :] + cb_ref[...]
    xc_b = xc.astype(BF16)
    rs, xs = [], []
    for n in range(LRU_BLOCKS):
        g = _dot(xc_b[:, n * LRU_BS:(n + 1) * LRU_BS], wax_ref[n])
        rs.append(g[:, :LRU_BS])
        xs.append(g[:, LRU_BS:])
    r = _sigmoid(jnp.concatenate(rs, axis=1) + ba_ref[...])
    i = _sigmoid(jnp.concatenate(xs, axis=1) + bx_ref[...])
    log_a = (-LRU_C) * r * _softplus(-lam_ref[...])
    a_ref[...] = jnp.exp(log_a)
    th = jnp.tanh(log_a)
    u_ref[...] = jnp.sqrt(-2.0 * th / (1.0 - th)) * (i * xc)

    row8 = _iota2((SUBLANE, LRU_WIDTH), 0)

    def body(g, h):
        base = pl.multiple_of(g * SUBLANE, SUBLANE)
        a = a_ref[pl.ds(base, SUBLANE), :]
        u = u_ref[pl.ds(base, SUBLANE), :]
        s = 1
        while s < SUBLANE:
            keep = row8 >= s
            u = u + a * jnp.where(keep, pltpu.roll(u, s, axis=0), 0.0)
            a = a * jnp.where(keep, pltpu.roll(a, s, axis=0), 1.0)
            s *= 2
        hs = a * h + u
        hs_ref[pl.ds(base, SUBLANE), :] = hs
        return hs[SUBLANE - 1:SUBLANE, :]

    h = lax.fori_loop(0, TL // SUBLANE, body, h_ref[...], unroll=2)
    h_ref[...] = h
    hl_ref[0] = h
    y_ref[0] = (hs_ref[...] * _silu(z_ref[0].astype(F32))).astype(BF16)
    xpad_ref[0:PAD, :] = xpad_ref[TL:TL + PAD, :]


def _lru(proj, conv0_pad, h0, cw, cb, wax, ba, bx, lam):
    B, L, _ = proj.shape
    TL = min(L, MIXER_TL)
    W = LRU_WIDTH
    xb, zb = COL_XL // W, COL_ZLRU // W
    vec = pl.BlockSpec((1, W), lambda b, l: (0, 0))
    return pl.pallas_call(
        _lru_kernel,
        grid=(B, L // TL),
        in_specs=[
            pl.BlockSpec((1, TL, W), lambda b, l: (b, l, xb)),
            pl.BlockSpec((1, TL, W), lambda b, l: (b, l, zb)),
            pl.BlockSpec((1, SUBLANE, W), lambda b, l: (b, 0, 0)),
            pl.BlockSpec((1, 1, W), lambda b, l: (b, 0, 0)),
            pl.BlockSpec((CONV_WIDTH, W), lambda b, l: (0, 0)),
            vec,
            pl.BlockSpec((LRU_BLOCKS, LRU_BS, 2 * LRU_BS), lambda b, l: (0, 0, 0)),
            vec, vec, vec,
        ],
        out_specs=[
            pl.BlockSpec((1, TL, W), lambda b, l: (b, l, 0)),
            pl.BlockSpec((1, 1, W), lambda b, l: (b, 0, 0)),
        ],
        out_shape=[
            jax.ShapeDtypeStruct((B, L, W), BF16),
            jax.ShapeDtypeStruct((B, 1, W), F32),
        ],
        scratch_shapes=[
            pltpu.VMEM((TL + SUBLANE, W), F32),
            pltpu.VMEM((TL, W), F32),
            pltpu.VMEM((TL, W), F32),
            pltpu.VMEM((TL, W), F32),
            pltpu.VMEM((1, W), F32),
        ],
        compiler_params=_cparams(("parallel", "arbitrary")),
        name="rglru",
    )(proj, proj, conv0_pad, h0, cw, cb, wax, ba, bx, lam)


def _headsum(x, ones_bd):
    xb = x.astype(BF16)
    outs = [_dot(xb[:, g * RWKV_GW:(g + 1) * RWKV_GW], ones_bd) for g in range(RWKV_NGROUPS)]
    return jnp.concatenate(outs, axis=1)


def _rwkv_local(loads, C, masks):
    tril_b, strict_rb, incl_rb, eye_rb, bd_sq, bd_cv, bd_state = masks

    def tile_rows(x, n):
        return jnp.concatenate([x] * n, axis=0)

    def blockdiag_cv(x):
        return tile_rows(x.astype(BF16), RWKV_GROUP) * bd_cv

    def blockdiag_sq(x):
        return tile_rows(x.astype(BF16), RWKV_GROUP) * bd_sq

    def stage_decay(load):
        r, k, v, kk, a, lw = load()
        lc = _cumsum_rows(tril_b, lw)
        wend = jnp.exp(lc[C - 1:C, :])
        einv = jnp.exp(-lc)
        bt = kk * jnp.exp(lc - lw)
        rt = r * jnp.exp(lc)
        at = -(kk * a) * einv
        kt = k * einv
        return dict(v=v, wend=wend, bt=bt, rt=rt, at=at, kt=kt)

    def stage_scores(p):
        br = jnp.concatenate([p['bt'], p['rt']], axis=0).astype(BF16)
        pa = _dot_nt(br, blockdiag_cv(p['at']))
        pk = _dot_nt(br, blockdiag_cv(p['kt']))
        a_ba = jnp.where(strict_rb, pa[:C], 0.0)
        p.update(a_ra=jnp.where(incl_rb, pa[C:], 0.0).astype(BF16),
                 a_bk=jnp.where(strict_rb, pk[:C], 0.0).astype(BF16),
                 a_rk=jnp.where(incl_rb, pk[C:], 0.0).astype(BF16),
                 x=a_ba, xb=blockdiag_sq(a_ba), t=eye_rb + a_ba)
        return p

    def stage_values(p):
        xv = _dot(jnp.concatenate([p['a_bk'], p['a_rk']], axis=0), blockdiag_cv(p['v']))
        p.update(xv=xv[:C], yv=xv[C:])
        return p

    def stage_square(p):
        p['x'] = _dot(p['x'].astype(BF16), p['xb'])
        p['xb'] = blockdiag_sq(p['x'])
        return p

    def stage_double(p, last):
        if last:
            p['t'] = p['t'] + _dot(p['t'].astype(BF16), p['xb'])
            return p
        xt = _dot(jnp.concatenate([p['x'], p['t']], axis=0).astype(BF16), p['xb'])
        p['x'] = xt[:C]
        p['t'] = p['t'] + xt[C:]
        p['xb'] = blockdiag_sq(p['x'])
        return p

    def stage_apply(p):
        tb = p['t'].astype(BF16)
        p.update(bhat=_dot(tb, blockdiag_cv(p['bt'])), uloc=_dot(tb, blockdiag_cv(p['xv'])))
        return p

    def stage_out(p):
        rhat = p['rt'] + _dot(p['a_ra'], blockdiag_cv(p['bhat']))
        yloc = _dot(p['a_ra'], blockdiag_cv(p['uloc'])) + p['yv']
        atw = (p['at'] * p['wend']).astype(BF16)
        ktw = (p['kt'] * p['wend']).astype(BF16)
        ghat = jnp.where(bd_state, _dot_tn(p['bhat'].astype(BF16), atw), 0.0)
        h = jnp.where(bd_state,
                      _dot_tn(jnp.concatenate([p['uloc'], p['v']], axis=0).astype(BF16),
                              jnp.concatenate([atw, ktw], axis=0)), 0.0)
        return rhat, yloc, ghat, h, p['wend']

    ps = [stage_decay(ld) for ld in loads]
    ps = [stage_scores(p) for p in ps]
    ps = [stage_values(p) for p in ps]
    ps = [stage_square(p) for p in ps]
    n = 4
    while n <= C:
        ps = [stage_double(p, last=(n == C)) for p in ps]
        n *= 2
    ps = [stage_apply(p) for p in ps]
    return [stage_out(p) for p in ps]


def _rwkv_masks(C):
    G = RWKV_GROUP
    tril = (_iota2((C, 3 * C), 0) >= (_iota2((C, 3 * C), 1) % C)).astype(BF16)
    r_rb = _iota2((C, G * C), 0)
    c_rb = _iota2((C, G * C), 1) % C
    strict_rb = r_rb > c_rb
    incl_rb = r_rb >= c_rb
    eye_rb = (r_rb == c_rb).astype(F32)
    bd_sq = ((_iota2((G * C, G * C), 0) // C) == (_iota2((G * C, G * C), 1) // C)).astype(BF16)
    bd_cv = ((_iota2((G * C, RWKV_GW), 0) // C) == (_iota2((G * C, RWKV_GW), 1) // RWKV_HD)).astype(BF16)
    bd_state = (_iota2((RWKV_GW, RWKV_GW), 0) // RWKV_HD) == (_iota2((RWKV_GW, RWKV_GW), 1) // RWKV_HD)
    return tril, strict_rb, incl_rb, eye_rb, bd_sq, bd_cv, bd_state


def _rwkv_kernel(r_ref, k_ref, v_ref, wa_ref, z_ref, sh_rkv_ref, sh_wa_ref, s0_ref,
                 mu_rkv_ref, mu_wa_ref, w0_ref, wup_ref, a0_ref, aup_ref, kk_ref, ka_ref, rk_ref,
                 gnw_ref, gnb_ref,
                 y_ref, s_ref,
                 xpad_ref, wapad_ref, st_ref, rs_ref, ks_ref, vs_ref, kks_ref, as_ref, lws_ref, ys_ref,
                 rhat_ref, ghat_ref, h_ref, wend_ref, *, C):
    li = pl.program_id(1)
    TL = r_ref.shape[1]
    PAD = SUBLANE
    W = RWKV_WIDTH

    @pl.when(li == 0)
    def _():
        for n in range(3):
            xpad_ref[n, PAD - 1:PAD, :] = sh_rkv_ref[0, :, n * W:(n + 1) * W]
        wapad_ref[PAD - 1:PAD, :] = sh_wa_ref[0]
        zero = jnp.zeros((RWKV_HD, RWKV_HD), F32)
        for g in range(RWKV_NGROUPS):
            st_ref[g] = jnp.concatenate(
                [jnp.concatenate([s0_ref[0, g * RWKV_GROUP + i] if i == j else zero
                                  for i in range(RWKV_GROUP)], axis=1)
                 for j in range(RWKV_GROUP)], axis=0)

    mixed = []
    for n, ref in enumerate((r_ref, k_ref, v_ref)):
        x = ref[0].astype(F32)
        xpad_ref[n, PAD:PAD + TL, :] = x
        prev = xpad_ref[n, PAD - 1:PAD - 1 + TL, :]
        mixed.append(x + (prev - x) * mu_rkv_ref[:, n * W:(n + 1) * W])
        xpad_ref[n, PAD - 1:PAD, :] = xpad_ref[n, PAD + TL - 1:PAD + TL, :]
    r, k, v = mixed
    xwa = wa_ref[0].astype(F32)
    wapad_ref[PAD:PAD + TL, :] = xwa
    prev = wapad_ref[PAD - 1:PAD - 1 + TL, :]
    wapad_ref[PAD - 1:PAD, :] = wapad_ref[PAD + TL - 1:PAD + TL, :]
    xwa = xwa + (prev - xwa) * mu_wa_ref[...]

    w = w0_ref[...] + _dot(jnp.tanh(xwa).astype(BF16), wup_ref[...])
    lw = (-RWKV_DECAY_SCALE) * _sigmoid(w)
    a = _sigmoid(a0_ref[...] + _dot(xwa.astype(BF16), aup_ref[...]))

    ones_bd = ((_iota2((RWKV_GW, RWKV_GW), 0) // RWKV_HD)
               == (_iota2((RWKV_GW, RWKV_GW), 1) // RWKV_HD)).astype(BF16)
    kk = k * kk_ref[...]
    kk = kk * lax.rsqrt(_headsum(kk * kk, ones_bd) + 1e-12)
    k = k * (1.0 + (a - 1.0) * ka_ref[...])
    rs_ref[...] = r
    ks_ref[...] = k
    vs_ref[...] = v
    kks_ref[...] = kk
    as_ref[...] = a
    lws_ref[...] = lw

    masks = _rwkv_masks(C)

    NCH = TL // C
    NB = min(NCH, RWKV_LOCAL_CHUNKS)

    def loader(rows, cols):
        return lambda: tuple(ref[rows, cols]
                             for ref in (rs_ref, ks_ref, vs_ref, kks_ref, as_ref, lws_ref))

    for cb in range(NCH // NB):
        probs = []
        for c in range(cb * NB, (cb + 1) * NB):
            rows = slice(c * C, (c + 1) * C)
            for g in range(RWKV_NGROUPS):
                probs.append((c, rows, g, slice(g * RWKV_GW, (g + 1) * RWKV_GW)))
        outs = _rwkv_local([loader(rows, cols) for _, rows, _, cols in probs], C, masks)
        for (c, rows, g, cols), (rhat, yloc, ghat, h, wend) in zip(probs, outs):
            rhat_ref[rows, cols] = rhat.astype(BF16)
            ys_ref[rows, cols] = yloc
            ghat_ref[c, g] = ghat.astype(BF16)
            h_ref[c, g] = h
            wend_ref[c, :, cols] = wend

    states = [st_ref[g] for g in range(RWKV_NGROUPS)]
    for c in range(NCH):
        rows = slice(c * C, (c + 1) * C)
        for g in range(RWKV_NGROUPS):
            cols = slice(g * RWKV_GW, (g + 1) * RWKV_GW)
            sb = states[g].astype(BF16)
            ys_ref[rows, cols] = ys_ref[rows, cols] + _dot_nt(rhat_ref[rows, cols], sb)
            states[g] = states[g] * wend_ref[c, :, cols] + _dot(sb, ghat_ref[c, g]) + h_ref[c, g]
    for g in range(RWKV_NGROUPS):
        st_ref[g] = states[g]

    y = ys_ref[...]
    r = rs_ref[...]
    k = ks_ref[...]
    v = vs_ref[...]
    inv_hd = 1.0 / RWKV_HD
    mean = _headsum(y, ones_bd) * inv_hd
    d = y - mean
    var = _headsum(d * d, ones_bd) * inv_hd
    yn = d * lax.rsqrt(var + RWKV_GN_EPS) * gnw_ref[...] + gnb_ref[...]
    yn = yn + _headsum(r * k * rk_ref[...], ones_bd) * v
    y_ref[0] = (yn * _silu(z_ref[0].astype(F32))).astype(BF16)

    @pl.when(li == pl.num_programs(1) - 1)
    def _():
        for g in range(RWKV_NGROUPS):
            for j in range(RWKV_GROUP):
                d = slice(j * RWKV_HD, (j + 1) * RWKV_HD)
                s_ref[0, g * RWKV_GROUP + j] = states[g][d, d]


def _rwkv(proj, sh_rkv, sh_wa, s0, mu_rkv, mu_wa, w0, wup_pad, a0, aup_pad, k_k, k_a, r_k, gn_w, gn_b, C):
    B, L, _ = proj.shape
    TL = min(L, MIXER_TL)
    W = RWKV_WIDTH
    rb, kb, vb, zb = COL_R // W, COL_KRW // W, COL_VRW // W, COL_ZRW // W
    wab = COL_WA // LANE
    vec = pl.BlockSpec((1, W), lambda b, l: (0, 0))
    mat = pl.BlockSpec((LANE, W), lambda b, l: (0, 0))
    tok = lambda cb: pl.BlockSpec((1, TL, W), lambda b, l: (b, l, cb))
    st_spec = pl.BlockSpec((1, RWKV_HEADS, RWKV_HD, RWKV_HD), lambda b, l: (b, 0, 0, 0))
    return pl.pallas_call(
        functools.partial(_rwkv_kernel, C=C),
        grid=(B, L // TL),
        in_specs=[
            tok(rb), tok(kb), tok(vb),
            pl.BlockSpec((1, TL, LANE), lambda b, l: (b, l, wab)),
            tok(zb),
            pl.BlockSpec((1, 1, 3 * W), lambda b, l: (b, 0, 0)),
            pl.BlockSpec((1, 1, LANE), lambda b, l: (b, 0, 0)),
            st_spec,
            pl.BlockSpec((1, 3 * W), lambda b, l: (0, 0)),
            pl.BlockSpec((1, LANE), lambda b, l: (0, 0)),
            vec, mat, vec, mat, vec, vec, vec, vec, vec,
        ],
        out_specs=[
            pl.BlockSpec((1, TL, W), lambda b, l: (b, l, 0)),
            st_spec,
        ],
        out_shape=[
            jax.ShapeDtypeStruct((B, L, W), BF16),
            jax.ShapeDtypeStruct((B, RWKV_HEADS, RWKV_HD, RWKV_HD), F32),
        ],
        scratch_shapes=[
            pltpu.VMEM((3, TL + SUBLANE, W), F32),
            pltpu.VMEM((TL + SUBLANE, LANE), F32),
            pltpu.VMEM((RWKV_NGROUPS, RWKV_GW, RWKV_GW), F32),
        ] + [pltpu.VMEM((TL, W), F32)] * 7 + [
            pltpu.VMEM((TL, W), BF16),
            pltpu.VMEM((TL // C, RWKV_NGROUPS, RWKV_GW, RWKV_GW), BF16),
            pltpu.VMEM((TL // C, RWKV_NGROUPS, RWKV_GW, RWKV_GW), F32),
            pltpu.VMEM((TL // C, 1, W), F32),
        ],
        compiler_params=_cparams(("parallel", "arbitrary")),
        name="rwkv7",
    )(proj, proj, proj, proj, proj, sh_rkv, sh_wa, s0,
      mu_rkv, mu_wa, w0, wup_pad, a0, aup_pad, k_k, k_a, r_k, gn_w, gn_b)


def _outproj_kernel(x_ref, yg_ref, yl_ref, yr_ref, ga_ref, gb_ref, gc_ref,
                    wg_ref, wl_ref, wr_ref, wo_ref, fg_ref, o_ref, *, final):
    merged = (_sigmoid(ga_ref[...].astype(F32)) * _dot(yg_ref[...], wg_ref[...])
              + _sigmoid(gb_ref[...].astype(F32)) * _dot(yl_ref[...], wl_ref[...])
              + _sigmoid(gc_ref[...].astype(F32)) * _dot(yr_ref[...], wr_ref[...]))
    x = x_ref[...] + _dot(merged.astype(BF16), wo_ref[...])
    if final:
        x = x * lax.rsqrt(jnp.mean(x * x, axis=-1, keepdims=True) + NORM_EPS) * fg_ref[...]
    o_ref[...] = x


def _outproj(x2d, yg, yl, yr, proj2d, wg, wl, wr, wo, fg, final):
    T = x2d.shape[0]
    tm = min(T, OUTPROJ_TM)
    D = D_MODEL
    gb = COL_GATES // D
    tok = lambda cb: pl.BlockSpec((tm, D), lambda i: (i, cb))
    wspec = pl.BlockSpec((D, D), lambda i: (0, 0))
    return pl.pallas_call(
        functools.partial(_outproj_kernel, final=final),
        grid=(T // tm,),
        in_specs=[tok(0), tok(0), tok(0), tok(0), tok(gb), tok(gb + 1), tok(gb + 2),
                  wspec, wspec, wspec, wspec, pl.BlockSpec((1, D), lambda i: (0, 0))],
        out_specs=tok(0),
        out_shape=jax.ShapeDtypeStruct((T, D), F32),
        compiler_params=_cparams(("parallel",)),
        name="outproj",
    )(x2d, yg, yl, yr, proj2d, proj2d, proj2d, wg, wl, wr, wo, fg)


def _prep_layer(P, l):
    w_in = P['w_in'][l]
    o_gd = GLA_DK + GLA_DK + GLA_DV
    o_zg = o_gd + GLA_RANK
    o_wa = o_zg + GLA_DV + 2 * LRU_WIDTH + 3 * RWKV_WIDTH
    o_zr = o_wa + DECAY_RANK + AAA_RANK
    w = jnp.concatenate([
        w_in[:, :o_gd].astype(BF16),
        w_in[:, o_zg:o_wa].astype(BF16),
        w_in[:, o_zr:].astype(BF16),
        w_in[:, o_gd:o_zg].astype(BF16),
        jnp.zeros((D_MODEL, LANE - GLA_RANK), BF16),
        w_in[:, o_wa:o_zr].astype(BF16),
    ], axis=1)
    zpad = jnp.zeros((LANE - GLA_RANK, GLA_DK), F32)
    z64 = jnp.zeros((DECAY_RANK, RWKV_WIDTH), F32)
    mu = P['rwkv_mu'][l]
    row = lambda a: a.reshape(1, -1)
    return dict(
        norm_g=row(P['norm_g'][l]), w_in=w,
        wg_pad=jnp.concatenate([P['gla_w_gup'][l], zpad], axis=0).astype(BF16),
        bg=row(P['gla_b_g'][l]), ng=row(P['gla_norm_g'][l]),
        cw=P['lru_conv_w'][l], cb=row(P['lru_conv_b'][l]),
        wax=jnp.concatenate([P['lru_w_a'][l], P['lru_w_x'][l]], axis=-1).astype(BF16),
        ba=row(P['lru_b_a'][l]), bx=row(P['lru_b_x'][l]), lam=row(P['lru_lambda'][l]),
        mu_rkv=row(mu[:3 * RWKV_WIDTH]), mu_wa=row(mu[3 * RWKV_WIDTH:]),
        w0=row(P['rwkv_w0'][l]),
        wup_pad=jnp.concatenate([P['rwkv_w_up'][l], z64], axis=0).astype(BF16),
        a0=row(P['rwkv_a0'][l]),
        aup_pad=jnp.concatenate([z64, P['rwkv_a_up'][l]], axis=0).astype(BF16),
        k_k=row(P['rwkv_k_k'][l]), k_a=row(P['rwkv_k_a'][l]), r_k=row(P['rwkv_r_k'][l]),
        gn_w=row(P['rwkv_gn_w'][l]), gn_b=row(P['rwkv_gn_b'][l]),
        wpg=P['w_proj_gla'][l].astype(BF16), wpl=P['w_proj_lru'][l].astype(BF16),
        wpr=P['w_proj_rwkv'][l].astype(BF16), wo=P['w_out'][l].astype(BF16),
    )


def _to_kernel_states(gla, lru_h, lru_conv, rwkv, shift):
    Bn = gla.shape[1]
    conv_pad = jnp.zeros((DEPTH, Bn, SUBLANE - (CONV_WIDTH - 1), LRU_WIDTH), lru_conv.dtype)
    return (gla, lru_h[:, :, None, :], jnp.concatenate([conv_pad, lru_conv], axis=2), rwkv,
            shift[:, :, None, :3 * RWKV_WIDTH], shift[:, :, None, 3 * RWKV_WIDTH:])


def _zero_kernel_states(Bn, dt):
    z = lambda *shape: jnp.zeros((DEPTH, Bn) + shape, dt)
    return (z(GLA_HEADS, GLA_HK, GLA_HV), z(1, LRU_WIDTH), z(SUBLANE, LRU_WIDTH),
            z(RWKV_HEADS, RWKV_HD, RWKV_HD), z(1, 3 * RWKV_WIDTH), z(1, LANE))


def _run_trunk(x, states, layers, final_g, C):
    gla0, lru_h0, conv0_pad, rwkv0, sh_rkv, sh_wa = states
    B, L, D = x.shape
    T = B * L
    x2d = x.reshape(T, D)
    n_gla, n_h, n_conv, n_rw, n_shift = [], [], [], [], []
    for l in range(DEPTH):
        p = layers[l]
        tiles_per_seq = L // INPROJ_TM if L % INPROJ_TM == 0 else 0
        proj2d, tails = _inproj(x2d, p['norm_g'], p['w_in'], BF16 if tiles_per_seq else F32)
        proj = proj2d.reshape(B, L, D_PROJ)
        if tiles_per_seq:
            tail = tails.reshape(B, tiles_per_seq, SUBLANE, D_PROJ)[:, -1]
        else:
            tail = proj[:, L - SUBLANE:, :]

        yg, s_gla = _gla(proj, p['wg_pad'], p['bg'], p['ng'], gla0[l], C)
        yl, h_last = _lru(proj, conv0_pad[l], lru_h0[l], p['cw'], p['cb'], p['wax'],
                          p['ba'], p['bx'], p['lam'])
        yr, s_rw = _rwkv(proj, sh_rkv[l], sh_wa[l], rwkv0[l],
                         p['mu_rkv'], p['mu_wa'], p['w0'], p['wup_pad'], p['a0'], p['aup_pad'],
                         p['k_k'], p['k_a'], p['r_k'], p['gn_w'], p['gn_b'], C)

        x2d = _outproj(x2d, yg.reshape(T, D), yl.reshape(T, D), yr.reshape(T, D), proj2d,
                       p['wpg'], p['wpl'], p['wpr'], p['wo'], final_g, final=(l == DEPTH - 1))

        n_gla.append(s_gla)
        n_h.append(h_last)
        n_conv.append(tail[:, SUBLANE - (CONV_WIDTH - 1):, COL_XL:COL_XL + LRU_WIDTH])
        n_rw.append(s_rw)
        n_shift.append(jnp.concatenate(
            [tail[:, -1, COL_R:COL_R + 3 * RWKV_WIDTH], tail[:, -1, COL_WA:COL_WA + 2 * DECAY_RANK]],
            axis=-1))
    return (x2d.reshape(B, L, D), jnp.stack(n_gla), jnp.stack(n_h)[:, :, 0, :],
            jnp.stack(n_conv), jnp.stack(n_rw), jnp.stack(n_shift))


def kernel(x_prompt, x_sample, state_gla, state_lru_h, state_lru_conv, state_rwkv, state_rwkv_shift,
           norm_g, w_in, gla_w_gup, gla_b_g, gla_norm_g,
           lru_conv_w, lru_conv_b, lru_w_a, lru_b_a, lru_w_x, lru_b_x, lru_lambda,
           rwkv_mu, rwkv_w0, rwkv_w_up, rwkv_a0, rwkv_a_up, rwkv_k_k, rwkv_k_a, rwkv_r_k,
           rwkv_gn_w, rwkv_gn_b, w_proj_gla, w_proj_lru, w_proj_rwkv, w_out, final_norm_g):
    P = dict(norm_g=norm_g, w_in=w_in, gla_w_gup=gla_w_gup, gla_b_g=gla_b_g, gla_norm_g=gla_norm_g,
             lru_conv_w=lru_conv_w, lru_conv_b=lru_conv_b, lru_w_a=lru_w_a, lru_b_a=lru_b_a,
             lru_w_x=lru_w_x, lru_b_x=lru_b_x, lru_lambda=lru_lambda,
             rwkv_mu=rwkv_mu, rwkv_w0=rwkv_w0, rwkv_w_up=rwkv_w_up, rwkv_a0=rwkv_a0,
             rwkv_a_up=rwkv_a_up, rwkv_k_k=rwkv_k_k, rwkv_k_a=rwkv_k_a, rwkv_r_k=rwkv_r_k,
             rwkv_gn_w=rwkv_gn_w, rwkv_gn_b=rwkv_gn_b, w_proj_gla=w_proj_gla, w_proj_lru=w_proj_lru,
             w_proj_rwkv=w_proj_rwkv, w_out=w_out)
    layers = [_prep_layer(P, l) for l in range(DEPTH)]
    final_g = final_norm_g.reshape(1, -1)
    out_p = _run_trunk(x_prompt, _zero_kernel_states(x_prompt.shape[0], x_prompt.dtype),
                       layers, final_g, C=64)
    Ls = x_sample.shape[1]
    out_s = _run_trunk(x_sample, _to_kernel_states(state_gla, state_lru_h, state_lru_conv,
                                                   state_rwkv, state_rwkv_shift),
                       layers, final_g, C=64 if Ls % 64 == 0 else Ls)
    return (out_p[0], out_s[0]) + tuple(out_p[1:]) + tuple(out_s[1:])
```

```python
import functools

import numpy as np
import jax
import jax.numpy as jnp
from jax import lax
from jax.experimental import pallas as pl
from jax.experimental.pallas import tpu as pltpu

F32 = jnp.float32
BF16 = jnp.bfloat16

D_MODEL = 1024
DEPTH = 4
CHUNK = 64
NORM_EPS = 1e-6
GLA_HEADS = 4
GLA_HK = 128
GLA_HV = 256
GLA_DK = GLA_HEADS * GLA_HK
GLA_DV = GLA_HEADS * GLA_HV
GLA_RANK = 16
GLA_TAU = 16.0
GLA_HEADS_PER_STEP = 4
LRU_WIDTH = 1024
LRU_BLOCKS = 8
LRU_BS = 128
CONV_WIDTH = 4
LRU_C = 8.0
RWKV_WIDTH = 1024
RWKV_HD = 64
RWKV_HEADS = 16
RWKV_GROUP = 2
RWKV_GW = RWKV_GROUP * RWKV_HD
RWKV_NGROUPS = RWKV_HEADS // RWKV_GROUP
RWKV_LOCAL_CHUNKS = 2
DECAY_RANK = 64
AAA_RANK = 64
RWKV_GN_EPS = 64e-5
RWKV_DECAY_SCALE = 0.6065306597126334
SHIFT_WIDTH = 3 * RWKV_WIDTH + DECAY_RANK + AAA_RANK

LANE = 128
SUBLANE = 8

COL_Q = 0
COL_K = 512
COL_V = 1024
COL_ZGLA = 2048
COL_XL = 3072
COL_ZLRU = 4096
COL_R = 5120
COL_KRW = 6144
COL_VRW = 7168
COL_ZRW = 8192
COL_GATES = 9216
COL_GD = 12288
COL_WA = 12416
D_PROJ = 12544
PROJ_TN = 1792
INPROJ_TM = 1024
OUTPROJ_TM = 256
MIXER_TL = 256
RELAYOUT_ROWS = 128

VMEM_LIMIT = 56 * 1024 * 1024


def _cparams(sem):
    return pltpu.CompilerParams(dimension_semantics=sem, vmem_limit_bytes=VMEM_LIMIT)


def _dot(a, b):
    return jnp.dot(a, b, preferred_element_type=F32)


def _dot_nt(a, b):
    return lax.dot_general(a, b, (((1,), (1,)), ((), ())), preferred_element_type=F32)


def _dot_tn(a, b):
    return lax.dot_general(a, b, (((0,), (0,)), ((), ())), preferred_element_type=F32)


def _sigmoid(x):
    return 1.0 / (1.0 + jnp.exp(-x))


def _silu(x):
    return x * _sigmoid(x)


def _softplus(x):
    return jnp.maximum(x, 0.0) + jnp.log1p(jnp.exp(-jnp.abs(x)))


def _iota2(shape, dim):
    return lax.broadcasted_iota(jnp.int32, shape, dim)


def _cumsum_rows(sel3, x):
    hi = x.astype(BF16)
    r1 = x - hi.astype(F32)
    mid = r1.astype(BF16)
    lo = (r1 - mid.astype(F32)).astype(BF16)
    return _dot(sel3, jnp.concatenate([hi, mid, lo], axis=0))


def _inproj_kernel(x_ref, g_ref, w_ref, o_ref, tail_ref, xn_ref):
    @pl.when(pl.program_id(1) == 0)
    def _():
        x = x_ref[...]
        y = x * lax.rsqrt(jnp.mean(x * x, axis=-1, keepdims=True) + NORM_EPS)
        xn_ref[...] = (y * g_ref[...]).astype(BF16)

    acc = _dot(xn_ref[...], w_ref[...])
    o_ref[...] = acc.astype(o_ref.dtype)
    tail_ref[...] = acc[acc.shape[0] - SUBLANE:, :]


def _inproj(x2d, g, w_all, layer, out_dtype):
    T = x2d.shape[0]
    tm = min(T, INPROJ_TM)
    return pl.pallas_call(
        _inproj_kernel,
        grid=(T // tm, D_PROJ // PROJ_TN),
        in_specs=[
            pl.BlockSpec((tm, D_MODEL), lambda i, j: (i, 0)),
            pl.BlockSpec((1, D_MODEL), lambda i, j: (0, 0)),
            pl.BlockSpec((None, D_MODEL, PROJ_TN), lambda i, j: (layer, 0, j)),
        ],
        out_specs=[
            pl.BlockSpec((tm, PROJ_TN), lambda i, j: (i, j)),
            pl.BlockSpec((SUBLANE, PROJ_TN), lambda i, j: (i, j)),
        ],
        out_shape=[
            jax.ShapeDtypeStruct((T, D_PROJ), out_dtype),
            jax.ShapeDtypeStruct((T // tm * SUBLANE, D_PROJ), F32),
        ],
        scratch_shapes=[pltpu.VMEM((tm, D_MODEL), BF16)],
        compiler_params=_cparams(("parallel", "arbitrary")),
        name="inproj",
    )(x2d, g, w_all)


def _gla_levels(C):
    out, s = [], C // 2
    while s >= 1:
        out.append(s)
        s //= 2
    return out


def _gla_select_matrix(C):
    r = np.arange(C)
    tril = (r[:, None] >= r[None, :])
    blocks = [tril]
    for s in _gla_levels(C):
        boundary = (r // (2 * s)) * (2 * s) + s - 1
        blocks.append(tril[boundary])
    sel = np.concatenate(blocks, axis=0)
    return jnp.asarray(np.concatenate([sel, sel, sel], axis=1), dtype=BF16)


def _gla_kernel(q_ref, k_ref, v_ref, z_ref, gd_ref, wg_ref, bg_ref, ng_ref, s0_ref, gsel_ref,
                y_ref, s_ref, st_ref, *, C):
    li = pl.program_id(2)
    HG = GLA_HEADS_PER_STEP

    @pl.when(li == 0)
    def _():
        for h in range(HG):
            st_ref[h] = s0_ref[0, h].T

    row = _iota2((C, C), 0)
    col = _iota2((C, C), 1)
    eye = row == col
    masks = [((row // (2 * s)) == (col // (2 * s))) & ((row % (2 * s)) >= s) & ((col % (2 * s)) < s)
             for s in _gla_levels(C)]
    gsel = gsel_ref[...]
    TL = q_ref.shape[1]
    probs = [(h, slice(c * C, (c + 1) * C)) for h in range(HG) for c in range(TL // C)]
    kcols = lambda h: slice(h * GLA_HK, (h + 1) * GLA_HK)
    vcols = lambda h: slice(h * GLA_HV, (h + 1) * GLA_HV)

    def stage_decay(prob):
        h, sl = prob
        x = _dot(gd_ref[0, sl, :].astype(BF16), wg_ref[:, kcols(h)]) + bg_ref[:, kcols(h)]
        la = (jnp.minimum(x, 0.0) - jnp.log1p(jnp.exp(-jnp.abs(x)))) * (1.0 / GLA_TAU)
        return dict(h=h, sl=sl, d_all=_cumsum_rows(gsel, la))

    def stage_scores(p):
        h, sl = p['h'], p['sl']
        q = q_ref[0, sl, kcols(h)].astype(F32) * (GLA_HK ** -0.5)
        k = k_ref[0, sl, kcols(h)].astype(F32)
        d_all = p['d_all']
        b = d_all[:C]
        bend = b[C - 1:C, :]
        a = jnp.where(eye, jnp.sum(q * k, axis=-1, keepdims=True), 0.0)
        for lvl, mask in enumerate(masks):
            d = b - d_all[(lvl + 1) * C:(lvl + 2) * C]
            qs = q * jnp.exp(jnp.minimum(d, 0.0))
            ks = k * jnp.exp(jnp.minimum(-d, 0.0))
            a = jnp.where(mask, _dot_nt(qs.astype(BF16), ks.astype(BF16)), a)
        return dict(h=h, sl=sl, a=a.astype(BF16), qd=(q * jnp.exp(b)).astype(BF16),
                    kd=(k * jnp.exp(bend - b)).astype(BF16), wend=jnp.exp(bend))

    def stage_values(p):
        v = v_ref[0, p['sl'], vcols(p['h'])].astype(BF16)
        p.update(o=_dot(p['a'], v), upd=_dot_tn(v, p['kd']))
        return p

    ps = [stage_decay(prob) for prob in probs]
    ps = [stage_scores(p) for p in ps]
    ps = [stage_values(p) for p in ps]

    sts = [st_ref[h] for h in range(HG)]
    for p in ps:
        h, sl = p['h'], p['sl']
        o = p['o'] + _dot_nt(p['qd'], sts[h].astype(BF16))
        sts[h] = sts[h] * p['wend'] + p['upd']
        o = o * lax.rsqrt(jnp.mean(o * o, axis=-1, keepdims=True) + NORM_EPS) * ng_ref[:, vcols(h)]
        y_ref[0, sl, vcols(h)] = (o * _silu(z_ref[0, sl, vcols(h)].astype(F32))).astype(BF16)
    for h in range(HG):
        st_ref[h] = sts[h]

    @pl.when(li == pl.num_programs(2) - 1)
    def _():
        for h in range(HG):
            s_ref[0, h] = sts[h].T


def _gla(proj, wg_pad, bg, ng, s0, C):
    B, L, _ = proj.shape
    TL = min(L, MIXER_TL)
    HG = GLA_HEADS_PER_STEP
    KW, VW = HG * GLA_HK, HG * GLA_HV
    qb, kb = COL_Q // KW, COL_K // KW
    vb, zb = COL_V // VW, COL_ZGLA // VW
    gb = COL_GD // LANE
    gsel = _gla_select_matrix(C)
    st_spec = pl.BlockSpec((1, HG, GLA_HK, GLA_HV), lambda b, h, l: (b, h, 0, 0))
    return pl.pallas_call(
        functools.partial(_gla_kernel, C=C),
        grid=(B, GLA_HEADS // HG, L // TL),
        in_specs=[
            pl.BlockSpec((1, TL, KW), lambda b, h, l: (b, l, qb + h)),
            pl.BlockSpec((1, TL, KW), lambda b, h, l: (b, l, kb + h)),
            pl.BlockSpec((1, TL, VW), lambda b, h, l: (b, l, vb + h)),
            pl.BlockSpec((1, TL, VW), lambda b, h, l: (b, l, zb + h)),
            pl.BlockSpec((1, TL, LANE), lambda b, h, l: (b, l, gb)),
            pl.BlockSpec((LANE, KW), lambda b, h, l: (0, h)),
            pl.BlockSpec((1, KW), lambda b, h, l: (0, h)),
            pl.BlockSpec((1, VW), lambda b, h, l: (0, h)),
            st_spec,
            pl.BlockSpec(gsel.shape, lambda b, h, l: (0, 0)),
        ],
        out_specs=[
            pl.BlockSpec((1, TL, VW), lambda b, h, l: (b, l, h)),
            st_spec,
        ],
        out_shape=[
            jax.ShapeDtypeStruct((B, L, GLA_DV), BF16),
            jax.ShapeDtypeStruct((B, GLA_HEADS, GLA_HK, GLA_HV), F32),
        ],
        scratch_shapes=[pltpu.VMEM((HG, GLA_HV, GLA_HK), F32)],
        compiler_params=_cparams(("parallel", "parallel", "arbitrary")),
        name="gla",
    )(proj, proj, proj, proj, proj, wg_pad, bg, ng, s0, gsel)


def _lru_kernel(x_ref, z_ref, c0_ref, h0_ref, cw_ref, cb_ref, wax_ref, ba_ref, bx_ref, lam_ref,
                y_ref, hl_ref, xpad_ref, a_ref, u_ref, hs_ref, h_ref):
    li = pl.program_id(1)
    TL = x_ref.shape[1]
    PAD = SUBLANE

    @pl.when(li == 0)
    def _():
        xpad_ref[0:PAD, :] = c0_ref[0]
        h_ref[...] = h0_ref[0]

    xpad_ref[PAD:PAD + TL, :] = x_ref[0].astype(F32)
    xp = xpad_ref[...]
    acc = xp * cw_ref[0:1, :]
    for t in range(1, CONV_WIDTH):
        acc = pltpu.roll(acc, 1, axis=0) + xp * cw_ref[t:t + 1, :]
    xc = acc[PAD:, :] + cb_ref[...]
    xc_b = xc.astype(BF16)
    rs, xs = [], []
    for n in range(LRU_BLOCKS):
        g = _dot(xc_b[:, n * LRU_BS:(n + 1) * LRU_BS], wax_ref[n])
        rs.append(g[:, :LRU_BS])
        xs.append(g[:, LRU_BS:])
    r = _sigmoid(jnp.concatenate(rs, axis=1) + ba_ref[...])
    i = _sigmoid(jnp.concatenate(xs, axis=1) + bx_ref[...])
    log_a = (-LRU_C) * r * _softplus(-lam_ref[...])
    a_ref[...] = jnp.exp(log_a)
    th = jnp.tanh(log_a)
    u_ref[...] = jnp.sqrt(-2.0 * th / (1.0 - th)) * (i * xc)

    row8 = _iota2((SUBLANE, LRU_WIDTH), 0)

    def body(g, h):
        base = pl.multiple_of(g * SUBLANE, SUBLANE)
        a = a_ref[pl.ds(base, SUBLANE), :]
        u = u_ref[pl.ds(base, SUBLANE), :]
        s = 1
        while s < SUBLANE:
            keep = row8 >= s
            u = u + a * jnp.where(keep, pltpu.roll(u, s, axis=0), 0.0)
            a = a * jnp.where(keep, pltpu.roll(a, s, axis=0), 1.0)
            s *= 2
        hs = a * h + u
        hs_ref[pl.ds(base, SUBLANE), :] = hs
        return hs[SUBLANE - 1:SUBLANE, :]

    h = lax.fori_loop(0, TL // SUBLANE, body, h_ref[...], unroll=2)
    h_ref[...] = h
    hl_ref[0] = h
    y_ref[0] = (hs_ref[...] * _silu(z_ref[0].astype(F32))).astype(BF16)
    xpad_ref[0:PAD, :] = xpad_ref[TL:TL + PAD, :]


def _lru(proj, conv0_pad, h0, cw, cb, wax, ba, bx, lam):
    B, L, _ = proj.shape
    TL = min(L, MIXER_TL)
    W = LRU_WIDTH
    xb, zb = COL_XL // W, COL_ZLRU // W
    vec = pl.BlockSpec((1, W), lambda b, l: (0, 0))
    return pl.pallas_call(
        _lru_kernel,
        grid=(B, L // TL),
        in_specs=[
            pl.BlockSpec((1, TL, W), lambda b, l: (b, l, xb)),
            pl.BlockSpec((1, TL, W), lambda b, l: (b, l, zb)),
            pl.BlockSpec((1, SUBLANE, W), lambda b, l: (b, 0, 0)),
            pl.BlockSpec((1, 1, W), lambda b, l: (b, 0, 0)),
            pl.BlockSpec((CONV_WIDTH, W), lambda b, l: (0, 0)),
            vec,
            pl.BlockSpec((LRU_BLOCKS, LRU_BS, 2 * LRU_BS), lambda b, l: (0, 0, 0)),
            vec, vec, vec,
        ],
        out_specs=[
            pl.BlockSpec((1, TL, W), lambda b, l: (b, l, 0)),
            pl.BlockSpec((1, 1, W), lambda b, l: (b, 0, 0)),
        ],
        out_shape=[
            jax.ShapeDtypeStruct((B, L, W), BF16),
            jax.ShapeDtypeStruct((B, 1, W), F32),
        ],
        scratch_shapes=[
            pltpu.VMEM((TL + SUBLANE, W), F32),
            pltpu.VMEM((TL, W), F32),
            pltpu.VMEM((TL, W), F32),
            pltpu.VMEM((TL, W), F32),
            pltpu.VMEM((1, W), F32),
        ],
        compiler_params=_cparams(("parallel", "arbitrary")),
        name="rglru",
    )(proj, proj, conv0_pad, h0, cw, cb, wax, ba, bx, lam)


def _headsum(x, ones_bd):
    xb = x.astype(BF16)
    outs = [_dot(xb[:, g * RWKV_GW:(g + 1) * RWKV_GW], ones_bd) for g in range(RWKV_NGROUPS)]
    return jnp.concatenate(outs, axis=1)


def _rwkv_local(loads, C, masks):
    tril_b, strict_rb, incl_rb, eye_rb, bd_sq, bd_cv, bd_state = masks

    def tile_rows(x, n):
        return jnp.concatenate([x] * n, axis=0)

    def blockdiag_cv(x):
        return tile_rows(x.astype(BF16), RWKV_GROUP) * bd_cv

    def blockdiag_sq(x):
        return tile_rows(x.astype(BF16), RWKV_GROUP) * bd_sq

    def stage_decay(load):
        r, k, v, kk, a, lw = load()
        lc = _cumsum_rows(tril_b, lw)
        wend = jnp.exp(lc[C - 1:C, :])
        einv = jnp.exp(-lc)
        bt = kk * jnp.exp(lc - lw)
        rt = r * jnp.exp(lc)
        at = -(kk * a) * einv
        kt = k * einv
        return dict(v=v, wend=wend, bt=bt, rt=rt, at=at, kt=kt)

    def stage_scores(p):
        br = jnp.concatenate([p['bt'], p['rt']], axis=0).astype(BF16)
        pa = _dot_nt(br, blockdiag_cv(p['at']))
        pk = _dot_nt(br, blockdiag_cv(p['kt']))
        a_ba = jnp.where(strict_rb, pa[:C], 0.0)
        p.update(a_ra=jnp.where(incl_rb, pa[C:], 0.0).astype(BF16),
                 a_bk=jnp.where(strict_rb, pk[:C], 0.0).astype(BF16),
                 a_rk=jnp.where(incl_rb, pk[C:], 0.0).astype(BF16),
                 x=a_ba, xb=blockdiag_sq(a_ba), t=eye_rb + a_ba)
        return p

    def stage_values(p):
        xv = _dot(jnp.concatenate([p['a_bk'], p['a_rk']], axis=0), blockdiag_cv(p['v']))
        p.update(xv=xv[:C], yv=xv[C:])
        return p

    def stage_square(p):
        p['x'] = _dot(p['x'].astype(BF16), p['xb'])
        p['xb'] = blockdiag_sq(p['x'])
        return p

    def stage_double(p, last):
        if last:
            p['t'] = p['t'] + _dot(p['t'].astype(BF16), p['xb'])
            return p
        xt = _dot(jnp.concatenate([p['x'], p['t']], axis=0).astype(BF16), p['xb'])
        p['x'] = xt[:C]
        p['t'] = p['t'] + xt[C:]
        p['xb'] = blockdiag_sq(p['x'])
        return p

    def stage_apply(p):
        tb = p['t'].astype(BF16)
        p.update(bhat=_dot(tb, blockdiag_cv(p['bt'])), uloc=_dot(tb, blockdiag_cv(p['xv'])))
        return p

    def stage_out(p):
        rhat = p['rt'] + _dot(p['a_ra'], blockdiag_cv(p['bhat']))
        yloc = _dot(p['a_ra'], blockdiag_cv(p['uloc'])) + p['yv']
        atw = (p['at'] * p['wend']).astype(BF16)
        ktw = (p['kt'] * p['wend']).astype(BF16)
        ghat = jnp.where(bd_state, _dot_tn(p['bhat'].astype(BF16), atw), 0.0)
        h = jnp.where(bd_state,
                      _dot_tn(jnp.concatenate([p['uloc'], p['v']], axis=0).astype(BF16),
                              jnp.concatenate([atw, ktw], axis=0)), 0.0)
        return rhat, yloc, ghat, h, p['wend']

    ps = [stage_decay(ld) for ld in loads]
    ps = [stage_scores(p) for p in ps]
    ps = [stage_values(p) for p in ps]
    ps = [stage_square(p) for p in ps]
    n = 4
    while n <= C:
        ps = [stage_double(p, last=(n == C)) for p in ps]
        n *= 2
    ps = [stage_apply(p) for p in ps]
    return [stage_out(p) for p in ps]


def _rwkv_masks(C):
    G = RWKV_GROUP
    tril = (_iota2((C, 3 * C), 0) >= (_iota2((C, 3 * C), 1) % C)).astype(BF16)
    r_rb = _iota2((C, G * C), 0)
    c_rb = _iota2((C, G * C), 1) % C
    strict_rb = r_rb > c_rb
    incl_rb = r_rb >= c_rb
    eye_rb = (r_rb == c_rb).astype(F32)
    bd_sq = ((_iota2((G * C, G * C), 0) // C) == (_iota2((G * C, G * C), 1) // C)).astype(BF16)
    bd_cv = ((_iota2((G * C, RWKV_GW), 0) // C) == (_iota2((G * C, RWKV_GW), 1) // RWKV_HD)).astype(BF16)
    bd_state = (_iota2((RWKV_GW, RWKV_GW), 0) // RWKV_HD) == (_iota2((RWKV_GW, RWKV_GW), 1) // RWKV_HD)
    return tril, strict_rb, incl_rb, eye_rb, bd_sq, bd_cv, bd_state


def _rwkv_kernel(r_ref, k_ref, v_ref, wa_ref, z_ref, sh_rkv_ref, sh_wa_ref, s0_ref,
                 mu_rkv_ref, mu_wa_ref, w0_ref, wup_ref, a0_ref, aup_ref, kk_ref, ka_ref, rk_ref,
                 gnw_ref, gnb_ref,
                 y_ref, s_ref,
                 xpad_ref, wapad_ref, st_ref, rs_ref, ks_ref, vs_ref, kks_ref, as_ref, lws_ref, ys_ref,
                 rhat_ref, ghat_ref, h_ref, wend_ref, *, C):
    li = pl.program_id(1)
    TL = r_ref.shape[1]
    PAD = SUBLANE
    W = RWKV_WIDTH

    @pl.when(li == 0)
    def _():
        for n in range(3):
            xpad_ref[n, PAD - 1:PAD, :] = sh_rkv_ref[0, :, n * W:(n + 1) * W]
        wapad_ref[PAD - 1:PAD, :] = sh_wa_ref[0]
        zero = jnp.zeros((RWKV_HD, RWKV_HD), F32)
        for g in range(RWKV_NGROUPS):
            st_ref[g] = jnp.concatenate(
                [jnp.concatenate([s0_ref[0, g * RWKV_GROUP + i] if i == j else zero
                                  for i in range(RWKV_GROUP)], axis=1)
                 for j in range(RWKV_GROUP)], axis=0)

    mixed = []
    for n, ref in enumerate((r_ref, k_ref, v_ref)):
        x = ref[0].astype(F32)
        xpad_ref[n, PAD:PAD + TL, :] = x
        prev = xpad_ref[n, PAD - 1:PAD - 1 + TL, :]
        mixed.append(x + (prev - x) * mu_rkv_ref[:, n * W:(n + 1) * W])
        xpad_ref[n, PAD - 1:PAD, :] = xpad_ref[n, PAD + TL - 1:PAD + TL, :]
    r, k, v = mixed
    xwa = wa_ref[0].astype(F32)
    wapad_ref[PAD:PAD + TL, :] = xwa
    prev = wapad_ref[PAD - 1:PAD - 1 + TL, :]
    wapad_ref[PAD - 1:PAD, :] = wapad_ref[PAD + TL - 1:PAD + TL, :]
    xwa = xwa + (prev - xwa) * mu_wa_ref[...]

    w = w0_ref[...] + _dot(jnp.tanh(xwa).astype(BF16), wup_ref[...])
    lw = (-RWKV_DECAY_SCALE) * _sigmoid(w)
    a = _sigmoid(a0_ref[...] + _dot(xwa.astype(BF16), aup_ref[...]))

    ones_bd = ((_iota2((RWKV_GW, RWKV_GW), 0) // RWKV_HD)
               == (_iota2((RWKV_GW, RWKV_GW), 1) // RWKV_HD)).astype(BF16)
    kk = k * kk_ref[...]
    kk = kk * lax.rsqrt(_headsum(kk * kk, ones_bd) + 1e-12)
    k = k * (1.0 + (a - 1.0) * ka_ref[...])
    rs_ref[...] = r
    ks_ref[...] = k
    vs_ref[...] = v
    kks_ref[...] = kk
    as_ref[...] = a
    lws_ref[...] = lw

    masks = _rwkv_masks(C)

    NCH = TL // C
    NB = min(NCH, RWKV_LOCAL_CHUNKS)

    def loader(rows, cols):
        return lambda: tuple(ref[rows, cols]
                             for ref in (rs_ref, ks_ref, vs_ref, kks_ref, as_ref, lws_ref))

    for cb in range(NCH // NB):
        probs = []
        for c in range(cb * NB, (cb + 1) * NB):
            rows = slice(c * C, (c + 1) * C)
            for g in range(RWKV_NGROUPS):
                probs.append((c, rows, g, slice(g * RWKV_GW, (g + 1) * RWKV_GW)))
        outs = _rwkv_local([loader(rows, cols) for _, rows, _, cols in probs], C, masks)
        for (c, rows, g, cols), (rhat, yloc, ghat, h, wend) in zip(probs, outs):
            rhat_ref[rows, cols] = rhat.astype(BF16)
            ys_ref[rows, cols] = yloc
            ghat_ref[c, g] = ghat.astype(BF16)
            h_ref[c, g] = h
            wend_ref[c, :, cols] = wend

    states = [st_ref[g] for g in range(RWKV_NGROUPS)]
    for c in range(NCH):
        rows = slice(c * C, (c + 1) * C)
        for g in range(RWKV_NGROUPS):
            cols = slice(g * RWKV_GW, (g + 1) * RWKV_GW)
            sb = states[g].astype(BF16)
            ys_ref[rows, cols] = ys_ref[rows, cols] + _dot_nt(rhat_ref[rows, cols], sb)
            states[g] = states[g] * wend_ref[c, :, cols] + _dot(sb, ghat_ref[c, g]) + h_ref[c, g]
    for g in range(RWKV_NGROUPS):
        st_ref[g] = states[g]

    y = ys_ref[...]
    r = rs_ref[...]
    k = ks_ref[...]
    v = vs_ref[...]
    inv_hd = 1.0 / RWKV_HD
    mean = _headsum(y, ones_bd) * inv_hd
    d = y - mean
    var = _headsum(d * d, ones_bd) * inv_hd
    yn = d * lax.rsqrt(var + RWKV_GN_EPS) * gnw_ref[...] + gnb_ref[...]
    yn = yn + _headsum(r * k * rk_ref[...], ones_bd) * v
    y_ref[0] = (yn * _silu(z_ref[0].astype(F32))).astype(BF16)

    @pl.when(li == pl.num_programs(1) - 1)
    def _():
        for g in range(RWKV_NGROUPS):
            for j in range(RWKV_GROUP):
                d = slice(j * RWKV_HD, (j + 1) * RWKV_HD)
                s_ref[0, g * RWKV_GROUP + j] = states[g][d, d]


def _rwkv(proj, sh_rkv, sh_wa, s0, mu_rkv, mu_wa, w0, wup_pad, a0, aup_pad, k_k, k_a, r_k, gn_w, gn_b, C):
    B, L, _ = proj.shape
    TL = min(L, MIXER_TL)
    W = RWKV_WIDTH
    rb, kb, vb, zb = COL_R // W, COL_KRW // W, COL_VRW // W, COL_ZRW // W
    wab = COL_WA // LANE
    vec = pl.BlockSpec((1, W), lambda b, l: (0, 0))
    mat = pl.BlockSpec((LANE, W), lambda b, l: (0, 0))
    tok = lambda cb: pl.BlockSpec((1, TL, W), lambda b, l: (b, l, cb))
    st_spec = pl.BlockSpec((1, RWKV_HEADS, RWKV_HD, RWKV_HD), lambda b, l: (b, 0, 0, 0))
    return pl.pallas_call(
        functools.partial(_rwkv_kernel, C=C),
        grid=(B, L // TL),
        in_specs=[
            tok(rb), tok(kb), tok(vb),
            pl.BlockSpec((1, TL, LANE), lambda b, l: (b, l, wab)),
            tok(zb),
            pl.BlockSpec((1, 1, 3 * W), lambda b, l: (b, 0, 0)),
            pl.BlockSpec((1, 1, LANE), lambda b, l: (b, 0, 0)),
            st_spec,
            pl.BlockSpec((1, 3 * W), lambda b, l: (0, 0)),
            pl.BlockSpec((1, LANE), lambda b, l: (0, 0)),
            vec, mat, vec, mat, vec, vec, vec, vec, vec,
        ],
        out_specs=[
            pl.BlockSpec((1, TL, W), lambda b, l: (b, l, 0)),
            st_spec,
        ],
        out_shape=[
            jax.ShapeDtypeStruct((B, L, W), BF16),
            jax.ShapeDtypeStruct((B, RWKV_HEADS, RWKV_HD, RWKV_HD), F32),
        ],
        scratch_shapes=[
            pltpu.VMEM((3, TL + SUBLANE, W), F32),
            pltpu.VMEM((TL + SUBLANE, LANE), F32),
            pltpu.VMEM((RWKV_NGROUPS, RWKV_GW, RWKV_GW), F32),
        ] + [pltpu.VMEM((TL, W), F32)] * 7 + [
            pltpu.VMEM((TL, W), BF16),
            pltpu.VMEM((TL // C, RWKV_NGROUPS, RWKV_GW, RWKV_GW), BF16),
            pltpu.VMEM((TL // C, RWKV_NGROUPS, RWKV_GW, RWKV_GW), F32),
            pltpu.VMEM((TL // C, 1, W), F32),
        ],
        compiler_params=_cparams(("parallel", "arbitrary")),
        name="rwkv7",
    )(proj, proj, proj, proj, proj, sh_rkv, sh_wa, s0,
      mu_rkv, mu_wa, w0, wup_pad, a0, aup_pad, k_k, k_a, r_k, gn_w, gn_b)


def _outproj_kernel(x_ref, yg_ref, yl_ref, yr_ref, ga_ref, gb_ref, gc_ref,
                    wg_ref, wl_ref, wr_ref, wo_ref, fg_ref, o_ref, *, final):
    merged = (_sigmoid(ga_ref[...].astype(F32)) * _dot(yg_ref[...], wg_ref[...])
              + _sigmoid(gb_ref[...].astype(F32)) * _dot(yl_ref[...], wl_ref[...])
              + _sigmoid(gc_ref[...].astype(F32)) * _dot(yr_ref[...], wr_ref[...]))
    x = x_ref[...] + _dot(merged.astype(BF16), wo_ref[...])
    if final:
        x = x * lax.rsqrt(jnp.mean(x * x, axis=-1, keepdims=True) + NORM_EPS) * fg_ref[...]
    o_ref[...] = x


def _outproj(x2d, yg, yl, yr, proj2d, wg, wl, wr, wo, fg, final):
    T = x2d.shape[0]
    tm = min(T, OUTPROJ_TM)
    D = D_MODEL
    gb = COL_GATES // D
    tok = lambda cb: pl.BlockSpec((tm, D), lambda i: (i, cb))
    wspec = pl.BlockSpec((D, D), lambda i: (0, 0))
    return pl.pallas_call(
        functools.partial(_outproj_kernel, final=final),
        grid=(T // tm,),
        in_specs=[tok(0), tok(0), tok(0), tok(0), tok(gb), tok(gb + 1), tok(gb + 2),
                  wspec, wspec, wspec, wspec, pl.BlockSpec((1, D), lambda i: (0, 0))],
        out_specs=tok(0),
        out_shape=jax.ShapeDtypeStruct((T, D), F32),
        compiler_params=_cparams(("parallel",)),
        name="outproj",
    )(x2d, yg, yl, yr, proj2d, proj2d, proj2d, wg, wl, wr, wo, fg)


def _relayout_kernel(w_ref, o_ref):
    w = w_ref[0]
    o_gd = GLA_DK + GLA_DK + GLA_DV
    o_zg = o_gd + GLA_RANK
    o_wa = o_zg + GLA_DV + 2 * LRU_WIDTH + 3 * RWKV_WIDTH
    o_zr = o_wa + DECAY_RANK + AAA_RANK
    o_ref[0] = jnp.concatenate([
        w[:, :o_gd],
        w[:, o_zg:o_wa],
        w[:, o_zr:],
        w[:, o_gd:o_zg],
        jnp.zeros((w.shape[0], LANE - GLA_RANK), F32),
        w[:, o_wa:o_zr],
    ], axis=1).astype(BF16)


def _relayout_w_in(w_in):
    d_in = w_in.shape[-1]
    return pl.pallas_call(
        _relayout_kernel,
        grid=(DEPTH, D_MODEL // RELAYOUT_ROWS),
        in_specs=[pl.BlockSpec((1, RELAYOUT_ROWS, d_in), lambda l, i: (l, i, 0))],
        out_specs=pl.BlockSpec((1, RELAYOUT_ROWS, D_PROJ), lambda l, i: (l, i, 0)),
        out_shape=jax.ShapeDtypeStruct((DEPTH, D_MODEL, D_PROJ), BF16),
        compiler_params=_cparams(("parallel", "parallel")),
        name="relayout_w_in",
    )(w_in)


def _prep_layer(P, l):
    zpad = jnp.zeros((LANE - GLA_RANK, GLA_DK), F32)
    z64 = jnp.zeros((DECAY_RANK, RWKV_WIDTH), F32)
    mu = P['rwkv_mu'][l]
    row = lambda a: a.reshape(1, -1)
    return dict(
        norm_g=row(P['norm_g'][l]),
        wg_pad=jnp.concatenate([P['gla_w_gup'][l], zpad], axis=0).astype(BF16),
        bg=row(P['gla_b_g'][l]), ng=row(P['gla_norm_g'][l]),
        cw=P['lru_conv_w'][l], cb=row(P['lru_conv_b'][l]),
        wax=jnp.concatenate([P['lru_w_a'][l], P['lru_w_x'][l]], axis=-1).astype(BF16),
        ba=row(P['lru_b_a'][l]), bx=row(P['lru_b_x'][l]), lam=row(P['lru_lambda'][l]),
        mu_rkv=row(mu[:3 * RWKV_WIDTH]), mu_wa=row(mu[3 * RWKV_WIDTH:]),
        w0=row(P['rwkv_w0'][l]),
        wup_pad=jnp.concatenate([P['rwkv_w_up'][l], z64], axis=0).astype(BF16),
        a0=row(P['rwkv_a0'][l]),
        aup_pad=jnp.concatenate([z64, P['rwkv_a_up'][l]], axis=0).astype(BF16),
        k_k=row(P['rwkv_k_k'][l]), k_a=row(P['rwkv_k_a'][l]), r_k=row(P['rwkv_r_k'][l]),
        gn_w=row(P['rwkv_gn_w'][l]), gn_b=row(P['rwkv_gn_b'][l]),
        wpg=P['w_proj_gla'][l].astype(BF16), wpl=P['w_proj_lru'][l].astype(BF16),
        wpr=P['w_proj_rwkv'][l].astype(BF16), wo=P['w_out'][l].astype(BF16),
    )


def _to_kernel_states(gla, lru_h, lru_conv, rwkv, shift):
    Bn = gla.shape[1]
    conv_pad = jnp.zeros((DEPTH, Bn, SUBLANE - (CONV_WIDTH - 1), LRU_WIDTH), lru_conv.dtype)
    return (gla, lru_h[:, :, None, :], jnp.concatenate([conv_pad, lru_conv], axis=2), rwkv,
            shift[:, :, None, :3 * RWKV_WIDTH], shift[:, :, None, 3 * RWKV_WIDTH:])


def _zero_kernel_states(Bn, dt):
    z = lambda *shape: jnp.zeros((DEPTH, Bn) + shape, dt)
    return (z(GLA_HEADS, GLA_HK, GLA_HV), z(1, LRU_WIDTH), z(SUBLANE, LRU_WIDTH),
            z(RWKV_HEADS, RWKV_HD, RWKV_HD), z(1, 3 * RWKV_WIDTH), z(1, LANE))


def _run_trunk(x, states, layers, w_all, final_g, C):
    gla0, lru_h0, conv0_pad, rwkv0, sh_rkv, sh_wa = states
    B, L, D = x.shape
    T = B * L
    x2d = x.reshape(T, D)
    n_gla, n_h, n_conv, n_rw, n_shift = [], [], [], [], []
    for l in range(DEPTH):
        p = layers[l]
        tiles_per_seq = L // INPROJ_TM if L % INPROJ_TM == 0 else 0
        proj2d, tails = _inproj(x2d, p['norm_g'], w_all, l, BF16 if tiles_per_seq else F32)
        proj = proj2d.reshape(B, L, D_PROJ)
        if tiles_per_seq:
            tail = tails.reshape(B, tiles_per_seq, SUBLANE, D_PROJ)[:, -1]
        else:
            tail = proj[:, L - SUBLANE:, :]

        yg, s_gla = _gla(proj, p['wg_pad'], p['bg'], p['ng'], gla0[l], C)
        yl, h_last = _lru(proj, conv0_pad[l], lru_h0[l], p['cw'], p['cb'], p['wax'],
                          p['ba'], p['bx'], p['lam'])
        yr, s_rw = _rwkv(proj, sh_rkv[l], sh_wa[l], rwkv0[l],
                         p['mu_rkv'], p['mu_wa'], p['w0'], p['wup_pad'], p['a0'], p['aup_pad'],
                         p['k_k'], p['k_a'], p['r_k'], p['gn_w'], p['gn_b'], C)

        x2d = _outproj(x2d, yg.reshape(T, D), yl.reshape(T, D), yr.reshape(T, D), proj2d,
                       p['wpg'], p['wpl'], p['wpr'], p['wo'], final_g, final=(l == DEPTH - 1))

        n_gla.append(s_gla)
        n_h.append(h_last)
        n_conv.append(tail[:, SUBLANE - (CONV_WIDTH - 1):, COL_XL:COL_XL + LRU_WIDTH])
        n_rw.append(s_rw)
        n_shift.append(jnp.concatenate(
            [tail[:, -1, COL_R:COL_R + 3 * RWKV_WIDTH], tail[:, -1, COL_WA:COL_WA + 2 * DECAY_RANK]],
            axis=-1))
    return (x2d.reshape(B, L, D), jnp.stack(n_gla), jnp.stack(n_h)[:, :, 0, :],
            jnp.stack(n_conv), jnp.stack(n_rw), jnp.stack(n_shift))


def kernel(x_prompt, x_sample, state_gla, state_lru_h, state_lru_conv, state_rwkv, state_rwkv_shift,
           norm_g, w_in, gla_w_gup, gla_b_g, gla_norm_g,
           lru_conv_w, lru_conv_b, lru_w_a, lru_b_a, lru_w_x, lru_b_x, lru_lambda,
           rwkv_mu, rwkv_w0, rwkv_w_up, rwkv_a0, rwkv_a_up, rwkv_k_k, rwkv_k_a, rwkv_r_k,
           rwkv_gn_w, rwkv_gn_b, w_proj_gla, w_proj_lru, w_proj_rwkv, w_out, final_norm_g):
    P = dict(norm_g=norm_g, w_in=w_in, gla_w_gup=gla_w_gup, gla_b_g=gla_b_g, gla_norm_g=gla_norm_g,
             lru_conv_w=lru_conv_w, lru_conv_b=lru_conv_b, lru_w_a=lru_w_a, lru_b_a=lru_b_a,
             lru_w_x=lru_w_x, lru_b_x=lru_b_x, lru_lambda=lru_lambda,
             rwkv_mu=rwkv_mu, rwkv_w0=rwkv_w0, rwkv_w_up=rwkv_w_up, rwkv_a0=rwkv_a0,
             rwkv_a_up=rwkv_a_up, rwkv_k_k=rwkv_k_k, rwkv_k_a=rwkv_k_a, rwkv_r_k=rwkv_r_k,
             rwkv_gn_w=rwkv_gn_w, rwkv_gn_b=rwkv_gn_b, w_proj_gla=w_proj_gla, w_proj_lru=w_proj_lru,
             w_proj_rwkv=w_proj_rwkv, w_out=w_out)
    layers = [_prep_layer(P, l) for l in range(DEPTH)]
    w_all = _relayout_w_in(w_in)
    final_g = final_norm_g.reshape(1, -1)
    out_p = _run_trunk(x_prompt, _zero_kernel_states(x_prompt.shape[0], x_prompt.dtype),
                       layers, w_all, final_g, C=CHUNK)
    Ls = x_sample.shape[1]
    out_s = _run_trunk(x_sample, _to_kernel_states(state_gla, state_lru_h, state_lru_conv,
                                                   state_rwkv, state_rwkv_shift),
                       layers, w_all, final_g, C=CHUNK if Ls % CHUNK == 0 else Ls)
    return (out_p[0], out_s[0]) + tuple(out_p[1:]) + tuple(out_s[1:])
```

```python
import functools

import numpy as np
import jax
import jax.numpy as jnp
from jax import lax
from jax.experimental import pallas as pl
from jax.experimental.pallas import tpu as pltpu

F32 = jnp.float32
BF16 = jnp.bfloat16

D_MODEL = 1024
DEPTH = 4
CHUNK = 64
NORM_EPS = 1e-6
GLA_HEADS = 4
GLA_HK = 128
GLA_HV = 256
GLA_DK = GLA_HEADS * GLA_HK
GLA_DV = GLA_HEADS * GLA_HV
GLA_RANK = 16
GLA_TAU = 16.0
GLA_HEADS_PER_STEP = 4
LRU_WIDTH = 1024
LRU_BLOCKS = 8
LRU_BS = 128
CONV_WIDTH = 4
LRU_C = 8.0
RWKV_WIDTH = 1024
RWKV_HD = 64
RWKV_HEADS = 16
RWKV_GROUP = 2
RWKV_GW = RWKV_GROUP * RWKV_HD
RWKV_NGROUPS = RWKV_HEADS // RWKV_GROUP
RWKV_LOCAL_CHUNKS = 2
DECAY_RANK = 64
AAA_RANK = 64
RWKV_GN_EPS = 64e-5
RWKV_DECAY_SCALE = 0.6065306597126334
SHIFT_WIDTH = 3 * RWKV_WIDTH + DECAY_RANK + AAA_RANK

LANE = 128
SUBLANE = 8

COL_Q = 0
COL_K = 512
COL_V = 1024
COL_ZGLA = 2048
COL_XL = 3072
COL_ZLRU = 4096
COL_R = 5120
COL_KRW = 6144
COL_VRW = 7168
COL_ZRW = 8192
COL_GATES = 9216
COL_GD = 12288
COL_WA = 12416
D_PROJ = 12544
PROJ_TN = 1792
INPROJ_TM = 1024
OUTPROJ_TM = 256
MIXER_TL = 256

VMEM_LIMIT = 56 * 1024 * 1024


def _cparams(sem):
    return pltpu.CompilerParams(dimension_semantics=sem, vmem_limit_bytes=VMEM_LIMIT)


def _dot(a, b):
    return jnp.dot(a, b, preferred_element_type=F32)


def _dot_nt(a, b):
    return lax.dot_general(a, b, (((1,), (1,)), ((), ())), preferred_element_type=F32)


def _dot_tn(a, b):
    return lax.dot_general(a, b, (((0,), (0,)), ((), ())), preferred_element_type=F32)


def _sigmoid(x):
    return 1.0 / (1.0 + jnp.exp(-x))


def _silu(x):
    return x * _sigmoid(x)


def _softplus(x):
    return jnp.maximum(x, 0.0) + jnp.log1p(jnp.exp(-jnp.abs(x)))


def _iota2(shape, dim):
    return lax.broadcasted_iota(jnp.int32, shape, dim)


def _cumsum_rows(sel3, x):
    hi = x.astype(BF16)
    r1 = x - hi.astype(F32)
    mid = r1.astype(BF16)
    lo = (r1 - mid.astype(F32)).astype(BF16)
    return _dot(sel3, jnp.concatenate([hi, mid, lo], axis=0))


def _inproj_kernel(x_ref, g_ref, w_ref, o_ref, tail_ref, xn_ref):
    @pl.when(pl.program_id(1) == 0)
    def _():
        x = x_ref[...]
        y = x * lax.rsqrt(jnp.mean(x * x, axis=-1, keepdims=True) + NORM_EPS)
        xn_ref[...] = (y * g_ref[...]).astype(BF16)

    acc = _dot_nt(xn_ref[...], w_ref[...])
    o_ref[...] = acc.astype(o_ref.dtype)
    tail_ref[...] = acc[acc.shape[0] - SUBLANE:, :]


def _inproj(x2d, g, w_all, layer, out_dtype):
    T = x2d.shape[0]
    tm = min(T, INPROJ_TM)
    return pl.pallas_call(
        _inproj_kernel,
        grid=(T // tm, D_PROJ // PROJ_TN),
        in_specs=[
            pl.BlockSpec((tm, D_MODEL), lambda i, j: (i, 0)),
            pl.BlockSpec((1, D_MODEL), lambda i, j: (0, 0)),
            pl.BlockSpec((None, PROJ_TN, D_MODEL), lambda i, j: (layer, j, 0)),
        ],
        out_specs=[
            pl.BlockSpec((tm, PROJ_TN), lambda i, j: (i, j)),
            pl.BlockSpec((SUBLANE, PROJ_TN), lambda i, j: (i, j)),
        ],
        out_shape=[
            jax.ShapeDtypeStruct((T, D_PROJ), out_dtype),
            jax.ShapeDtypeStruct((T // tm * SUBLANE, D_PROJ), F32),
        ],
        scratch_shapes=[pltpu.VMEM((tm, D_MODEL), BF16)],
        compiler_params=_cparams(("parallel", "arbitrary")),
        name="inproj",
    )(x2d, g, w_all)


def _gla_levels(C):
    out, s = [], C // 2
    while s >= 1:
        out.append(s)
        s //= 2
    return out


def _gla_select_matrix(C):
    r = np.arange(C)
    tril = (r[:, None] >= r[None, :])
    blocks = [tril]
    for s in _gla_levels(C):
        boundary = (r // (2 * s)) * (2 * s) + s - 1
        blocks.append(tril[boundary])
    sel = np.concatenate(blocks, axis=0)
    return jnp.asarray(np.concatenate([sel, sel, sel], axis=1), dtype=BF16)


def _gla_kernel(q_ref, k_ref, v_ref, z_ref, gd_ref, wg_ref, bg_ref, ng_ref, s0_ref, gsel_ref,
                y_ref, s_ref, st_ref, *, C):
    li = pl.program_id(2)
    HG = GLA_HEADS_PER_STEP

    @pl.when(li == 0)
    def _():
        for h in range(HG):
            st_ref[h] = s0_ref[0, h].T

    row = _iota2((C, C), 0)
    col = _iota2((C, C), 1)
    eye = row == col
    masks = [((row // (2 * s)) == (col // (2 * s))) & ((row % (2 * s)) >= s) & ((col % (2 * s)) < s)
             for s in _gla_levels(C)]
    gsel = gsel_ref[...]
    TL = q_ref.shape[1]
    probs = [(h, slice(c * C, (c + 1) * C)) for h in range(HG) for c in range(TL // C)]
    kcols = lambda h: slice(h * GLA_HK, (h + 1) * GLA_HK)
    vcols = lambda h: slice(h * GLA_HV, (h + 1) * GLA_HV)

    def stage_decay(prob):
        h, sl = prob
        x = _dot(gd_ref[0, sl, :].astype(BF16), wg_ref[:, kcols(h)]) + bg_ref[:, kcols(h)]
        la = (jnp.minimum(x, 0.0) - jnp.log1p(jnp.exp(-jnp.abs(x)))) * (1.0 / GLA_TAU)
        return dict(h=h, sl=sl, d_all=_cumsum_rows(gsel, la))

    def stage_scores(p):
        h, sl = p['h'], p['sl']
        q = q_ref[0, sl, kcols(h)].astype(F32) * (GLA_HK ** -0.5)
        k = k_ref[0, sl, kcols(h)].astype(F32)
        d_all = p['d_all']
        b = d_all[:C]
        bend = b[C - 1:C, :]
        a = jnp.where(eye, jnp.sum(q * k, axis=-1, keepdims=True), 0.0)
        for lvl, mask in enumerate(masks):
            d = b - d_all[(lvl + 1) * C:(lvl + 2) * C]
            qs = q * jnp.exp(jnp.minimum(d, 0.0))
            ks = k * jnp.exp(jnp.minimum(-d, 0.0))
            a = jnp.where(mask, _dot_nt(qs.astype(BF16), ks.astype(BF16)), a)
        return dict(h=h, sl=sl, a=a.astype(BF16), qd=(q * jnp.exp(b)).astype(BF16),
                    kd=(k * jnp.exp(bend - b)).astype(BF16), wend=jnp.exp(bend))

    def stage_values(p):
        v = v_ref[0, p['sl'], vcols(p['h'])].astype(BF16)
        p.update(o=_dot(p['a'], v), upd=_dot_tn(v, p['kd']))
        return p

    ps = [stage_decay(prob) for prob in probs]
    ps = [stage_scores(p) for p in ps]
    ps = [stage_values(p) for p in ps]

    sts = [st_ref[h] for h in range(HG)]
    for p in ps:
        h, sl = p['h'], p['sl']
        o = p['o'] + _dot_nt(p['qd'], sts[h].astype(BF16))
        sts[h] = sts[h] * p['wend'] + p['upd']
        o = o * lax.rsqrt(jnp.mean(o * o, axis=-1, keepdims=True) + NORM_EPS) * ng_ref[:, vcols(h)]
        y_ref[0, sl, vcols(h)] = (o * _silu(z_ref[0, sl, vcols(h)].astype(F32))).astype(BF16)
    for h in range(HG):
        st_ref[h] = sts[h]

    @pl.when(li == pl.num_programs(2) - 1)
    def _():
        for h in range(HG):
            s_ref[0, h] = sts[h].T


def _gla(proj, wg_pad, bg, ng, s0, C):
    B, L, _ = proj.shape
    TL = min(L, MIXER_TL)
    HG = GLA_HEADS_PER_STEP
    KW, VW = HG * GLA_HK, HG * GLA_HV
    qb, kb = COL_Q // KW, COL_K // KW
    vb, zb = COL_V // VW, COL_ZGLA // VW
    gb = COL_GD // LANE
    gsel = _gla_select_matrix(C)
    st_spec = pl.BlockSpec((1, HG, GLA_HK, GLA_HV), lambda b, h, l: (b, h, 0, 0))
    return pl.pallas_call(
        functools.partial(_gla_kernel, C=C),
        grid=(B, GLA_HEADS // HG, L // TL),
        in_specs=[
            pl.BlockSpec((1, TL, KW), lambda b, h, l: (b, l, qb + h)),
            pl.BlockSpec((1, TL, KW), lambda b, h, l: (b, l, kb + h)),
            pl.BlockSpec((1, TL, VW), lambda b, h, l: (b, l, vb + h)),
            pl.BlockSpec((1, TL, VW), lambda b, h, l: (b, l, zb + h)),
            pl.BlockSpec((1, TL, LANE), lambda b, h, l: (b, l, gb)),
            pl.BlockSpec((LANE, KW), lambda b, h, l: (0, h)),
            pl.BlockSpec((1, KW), lambda b, h, l: (0, h)),
            pl.BlockSpec((1, VW), lambda b, h, l: (0, h)),
            st_spec,
            pl.BlockSpec(gsel.shape, lambda b, h, l: (0, 0)),
        ],
        out_specs=[
            pl.BlockSpec((1, TL, VW), lambda b, h, l: (b, l, h)),
            st_spec,
        ],
        out_shape=[
            jax.ShapeDtypeStruct((B, L, GLA_DV), BF16),
            jax.ShapeDtypeStruct((B, GLA_HEADS, GLA_HK, GLA_HV), F32),
        ],
        scratch_shapes=[pltpu.VMEM((HG, GLA_HV, GLA_HK), F32)],
        compiler_params=_cparams(("parallel", "parallel", "arbitrary")),
        name="gla",
    )(proj, proj, proj, proj, proj, wg_pad, bg, ng, s0, gsel)


def _lru_kernel(x_ref, z_ref, c0_ref, h0_ref, cw_ref, cb_ref, wax_ref, ba_ref, bx_ref, lam_ref,
                y_ref, hl_ref, xpad_ref, a_ref, u_ref, hs_ref, h_ref):
    li = pl.program_id(1)
    TL = x_ref.shape[1]
    PAD = SUBLANE

    @pl.when(li == 0)
    def _():
        xpad_ref[0:PAD, :] = c0_ref[0]
        h_ref[...] = h0_ref[0]

    xpad_ref[PAD:PAD + TL, :] = x_ref[0].astype(F32)
    xp = xpad_ref[...]
    acc = xp * cw_ref[0:1, :]
    for t in range(1, CONV_WIDTH):
        acc = pltpu.roll(acc, 1, axis=0) + xp * cw_ref[t:t + 1, :]
    xc = acc[PAD:, :] + cb_ref[...]
    xc_b = xc.astype(BF16)
    rs, xs = [], []
    for n in range(LRU_BLOCKS):
        g = _dot(xc_b[:, n * LRU_BS:(n + 1) * LRU_BS], wax_ref[n])
        rs.append(g[:, :LRU_BS])
        xs.append(g[:, LRU_BS:])
    r = _sigmoid(jnp.concatenate(rs, axis=1) + ba_ref[...])
    i = _sigmoid(jnp.concatenate(xs, axis=1) + bx_ref[...])
    log_a = (-LRU_C) * r * _softplus(-lam_ref[...])
    a_ref[...] = jnp.exp(log_a)
    th = jnp.tanh(log_a)
    u_ref[...] = jnp.sqrt(-2.0 * th / (1.0 - th)) * (i * xc)

    row8 = _iota2((SUBLANE, LRU_WIDTH), 0)

    def body(g, h):
        base = pl.multiple_of(g * SUBLANE, SUBLANE)
        a = a_ref[pl.ds(base, SUBLANE), :]
        u = u_ref[pl.ds(base, SUBLANE), :]
        s = 1
        while s < SUBLANE:
            keep = row8 >= s
            u = u + a * jnp.where(keep, pltpu.roll(u, s, axis=0), 0.0)
            a = a * jnp.where(keep, pltpu.roll(a, s, axis=0), 1.0)
            s *= 2
        hs = a * h + u
        hs_ref[pl.ds(base, SUBLANE), :] = hs
        return hs[SUBLANE - 1:SUBLANE, :]

    h = lax.fori_loop(0, TL // SUBLANE, body, h_ref[...], unroll=2)
    h_ref[...] = h
    hl_ref[0] = h
    y_ref[0] = (hs_ref[...] * _silu(z_ref[0].astype(F32))).astype(BF16)
    xpad_ref[0:PAD, :] = xpad_ref[TL:TL + PAD, :]


def _lru(proj, conv0_pad, h0, cw, cb, wax, ba, bx, lam):
    B, L, _ = proj.shape
    TL = min(L, MIXER_TL)
    W = LRU_WIDTH
    xb, zb = COL_XL // W, COL_ZLRU // W
    vec = pl.BlockSpec((1, W), lambda b, l: (0, 0))
    return pl.pallas_call(
        _lru_kernel,
        grid=(B, L // TL),
        in_specs=[
            pl.BlockSpec((1, TL, W), lambda b, l: (b, l, xb)),
            pl.BlockSpec((1, TL, W), lambda b, l: (b, l, zb)),
            pl.BlockSpec((1, SUBLANE, W), lambda b, l: (b, 0, 0)),
            pl.BlockSpec((1, 1, W), lambda b, l: (b, 0, 0)),
            pl.BlockSpec((CONV_WIDTH, W), lambda b, l: (0, 0)),
            vec,
            pl.BlockSpec((LRU_BLOCKS, LRU_BS, 2 * LRU_BS), lambda b, l: (0, 0, 0)),
            vec, vec, vec,
        ],
        out_specs=[
            pl.BlockSpec((1, TL, W), lambda b, l: (b, l, 0)),
            pl.BlockSpec((1, 1, W), lambda b, l: (b, 0, 0)),
        ],
        out_shape=[
            jax.ShapeDtypeStruct((B, L, W), BF16),
            jax.ShapeDtypeStruct((B, 1, W), F32),
        ],
        scratch_shapes=[
            pltpu.VMEM((TL + SUBLANE, W), F32),
            pltpu.VMEM((TL, W), F32),
            pltpu.VMEM((TL, W), F32),
            pltpu.VMEM((TL, W), F32),
            pltpu.VMEM((1, W), F32),
        ],
        compiler_params=_cparams(("parallel", "arbitrary")),
        name="rglru",
    )(proj, proj, conv0_pad, h0, cw, cb, wax, ba, bx, lam)


def _headsum(x, ones_bd):
    xb = x.astype(BF16)
    outs = [_dot(xb[:, g * RWKV_GW:(g + 1) * RWKV_GW], ones_bd) for g in range(RWKV_NGROUPS)]
    return jnp.concatenate(outs, axis=1)


def _rwkv_local(loads, C, masks):
    tril_b, strict_rb, incl_rb, eye_rb, bd_sq, bd_cv, bd_state = masks

    def tile_rows(x, n):
        return jnp.concatenate([x] * n, axis=0)

    def blockdiag_cv(x):
        return tile_rows(x.astype(BF16), RWKV_GROUP) * bd_cv

    def blockdiag_sq(x):
        return tile_rows(x.astype(BF16), RWKV_GROUP) * bd_sq

    def stage_decay(load):
        r, k, v, kk, a, lw = load()
        lc = _cumsum_rows(tril_b, lw)
        wend = jnp.exp(lc[C - 1:C, :])
        einv = jnp.exp(-lc)
        bt = kk * jnp.exp(lc - lw)
        rt = r * jnp.exp(lc)
        at = -(kk * a) * einv
        kt = k * einv
        return dict(v=v, wend=wend, bt=bt, rt=rt, at=at, kt=kt)

    def stage_scores(p):
        br = jnp.concatenate([p['bt'], p['rt']], axis=0).astype(BF16)
        pa = _dot_nt(br, blockdiag_cv(p['at']))
        pk = _dot_nt(br, blockdiag_cv(p['kt']))
        a_ba = jnp.where(strict_rb, pa[:C], 0.0)
        p.update(a_ra=jnp.where(incl_rb, pa[C:], 0.0).astype(BF16),
                 a_bk=jnp.where(strict_rb, pk[:C], 0.0).astype(BF16),
                 a_rk=jnp.where(incl_rb, pk[C:], 0.0).astype(BF16),
                 x=a_ba, xb=blockdiag_sq(a_ba), t=eye_rb + a_ba)
        return p

    def stage_values(p):
        xv = _dot(jnp.concatenate([p['a_bk'], p['a_rk']], axis=0), blockdiag_cv(p['v']))
        p.update(xv=xv[:C], yv=xv[C:])
        return p

    def stage_square(p):
        p['x'] = _dot(p['x'].astype(BF16), p['xb'])
        p['xb'] = blockdiag_sq(p['x'])
        return p

    def stage_double(p, last):
        if last:
            p['t'] = p['t'] + _dot(p['t'].astype(BF16), p['xb'])
            return p
        xt = _dot(jnp.concatenate([p['x'], p['t']], axis=0).astype(BF16), p['xb'])
        p['x'] = xt[:C]
        p['t'] = p['t'] + xt[C:]
        p['xb'] = blockdiag_sq(p['x'])
        return p

    def stage_apply(p):
        tb = p['t'].astype(BF16)
        p.update(bhat=_dot(tb, blockdiag_cv(p['bt'])), uloc=_dot(tb, blockdiag_cv(p['xv'])))
        return p

    def stage_out(p):
        rhat = p['rt'] + _dot(p['a_ra'], blockdiag_cv(p['bhat']))
        yloc = _dot(p['a_ra'], blockdiag_cv(p['uloc'])) + p['yv']
        atw = (p['at'] * p['wend']).astype(BF16)
        ktw = (p['kt'] * p['wend']).astype(BF16)
        ghat = jnp.where(bd_state, _dot_tn(p['bhat'].astype(BF16), atw), 0.0)
        h = jnp.where(bd_state,
                      _dot_tn(jnp.concatenate([p['uloc'], p['v']], axis=0).astype(BF16),
                              jnp.concatenate([atw, ktw], axis=0)), 0.0)
        return rhat, yloc, ghat, h, p['wend']

    ps = [stage_decay(ld) for ld in loads]
    ps = [stage_scores(p) for p in ps]
    ps = [stage_values(p) for p in ps]
    ps = [stage_square(p) for p in ps]
    n = 4
    while n <= C:
        ps = [stage_double(p, last=(n == C)) for p in ps]
        n *= 2
    ps = [stage_apply(p) for p in ps]
    return [stage_out(p) for p in ps]


def _rwkv_masks(C):
    G = RWKV_GROUP
    tril = (_iota2((C, 3 * C), 0) >= (_iota2((C, 3 * C), 1) % C)).astype(BF16)
    r_rb = _iota2((C, G * C), 0)
    c_rb = _iota2((C, G * C), 1) % C
    strict_rb = r_rb > c_rb
    incl_rb = r_rb >= c_rb
    eye_rb = (r_rb == c_rb).astype(F32)
    bd_sq = ((_iota2((G * C, G * C), 0) // C) == (_iota2((G * C, G * C), 1) // C)).astype(BF16)
    bd_cv = ((_iota2((G * C, RWKV_GW), 0) // C) == (_iota2((G * C, RWKV_GW), 1) // RWKV_HD)).astype(BF16)
    bd_state = (_iota2((RWKV_GW, RWKV_GW), 0) // RWKV_HD) == (_iota2((RWKV_GW, RWKV_GW), 1) // RWKV_HD)
    return tril, strict_rb, incl_rb, eye_rb, bd_sq, bd_cv, bd_state


def _rwkv_kernel(r_ref, k_ref, v_ref, wa_ref, z_ref, sh_rkv_ref, sh_wa_ref, s0_ref,
                 mu_rkv_ref, mu_wa_ref, w0_ref, wup_ref, a0_ref, aup_ref, kk_ref, ka_ref, rk_ref,
                 gnw_ref, gnb_ref,
                 y_ref, s_ref,
                 xpad_ref, wapad_ref, st_ref, rs_ref, ks_ref, vs_ref, kks_ref, as_ref, lws_ref, ys_ref,
                 rhat_ref, ghat_ref, h_ref, wend_ref, *, C):
    li = pl.program_id(1)
    TL = r_ref.shape[1]
    PAD = SUBLANE
    W = RWKV_WIDTH

    @pl.when(li == 0)
    def _():
        for n in range(3):
            xpad_ref[n, PAD - 1:PAD, :] = sh_rkv_ref[0, :, n * W:(n + 1) * W]
        wapad_ref[PAD - 1:PAD, :] = sh_wa_ref[0]
        zero = jnp.zeros((RWKV_HD, RWKV_HD), F32)
        for g in range(RWKV_NGROUPS):
            st_ref[g] = jnp.concatenate(
                [jnp.concatenate([s0_ref[0, g * RWKV_GROUP + i] if i == j else zero
                                  for i in range(RWKV_GROUP)], axis=1)
                 for j in range(RWKV_GROUP)], axis=0)

    mixed = []
    for n, ref in enumerate((r_ref, k_ref, v_ref)):
        x = ref[0].astype(F32)
        xpad_ref[n, PAD:PAD + TL, :] = x
        prev = xpad_ref[n, PAD - 1:PAD - 1 + TL, :]
        mixed.append(x + (prev - x) * mu_rkv_ref[:, n * W:(n + 1) * W])
        xpad_ref[n, PAD - 1:PAD, :] = xpad_ref[n, PAD + TL - 1:PAD + TL, :]
    r, k, v = mixed
    xwa = wa_ref[0].astype(F32)
    wapad_ref[PAD:PAD + TL, :] = xwa
    prev = wapad_ref[PAD - 1:PAD - 1 + TL, :]
    wapad_ref[PAD - 1:PAD, :] = wapad_ref[PAD + TL - 1:PAD + TL, :]
    xwa = xwa + (prev - xwa) * mu_wa_ref[...]

    w = w0_ref[...] + _dot(jnp.tanh(xwa).astype(BF16), wup_ref[...])
    lw = (-RWKV_DECAY_SCALE) * _sigmoid(w)
    a = _sigmoid(a0_ref[...] + _dot(xwa.astype(BF16), aup_ref[...]))

    ones_bd = ((_iota2((RWKV_GW, RWKV_GW), 0) // RWKV_HD)
               == (_iota2((RWKV_GW, RWKV_GW), 1) // RWKV_HD)).astype(BF16)
    kk = k * kk_ref[...]
    kk = kk * lax.rsqrt(_headsum(kk * kk, ones_bd) + 1e-12)
    k = k * (1.0 + (a - 1.0) * ka_ref[...])
    rs_ref[...] = r
    ks_ref[...] = k
    vs_ref[...] = v
    kks_ref[...] = kk
    as_ref[...] = a
    lws_ref[...] = lw

    masks = _rwkv_masks(C)

    NCH = TL // C
    NB = min(NCH, RWKV_LOCAL_CHUNKS)

    def loader(rows, cols):
        return lambda: tuple(ref[rows, cols]
                             for ref in (rs_ref, ks_ref, vs_ref, kks_ref, as_ref, lws_ref))

    for cb in range(NCH // NB):
        probs = []
        for c in range(cb * NB, (cb + 1) * NB):
            rows = slice(c * C, (c + 1) * C)
            for g in range(RWKV_NGROUPS):
                probs.append((c, rows, g, slice(g * RWKV_GW, (g + 1) * RWKV_GW)))
        outs = _rwkv_local([loader(rows, cols) for _, rows, _, cols in probs], C, masks)
        for (c, rows, g, cols), (rhat, yloc, ghat, h, wend) in zip(probs, outs):
            rhat_ref[rows, cols] = rhat.astype(BF16)
            ys_ref[rows, cols] = yloc
            ghat_ref[c, g] = ghat.astype(BF16)
            h_ref[c, g] = h
            wend_ref[c, :, cols] = wend

    states = [st_ref[g] for g in range(RWKV_NGROUPS)]
    for c in range(NCH):
        rows = slice(c * C, (c + 1) * C)
        for g in range(RWKV_NGROUPS):
            cols = slice(g * RWKV_GW, (g + 1) * RWKV_GW)
            sb = states[g].astype(BF16)
            ys_ref[rows, cols] = ys_ref[rows, cols] + _dot_nt(rhat_ref[rows, cols], sb)
            states[g] = states[g] * wend_ref[c, :, cols] + _dot(sb, ghat_ref[c, g]) + h_ref[c, g]
    for g in range(RWKV_NGROUPS):
        st_ref[g] = states[g]

    y = ys_ref[...]
    r = rs_ref[...]
    k = ks_ref[...]
    v = vs_ref[...]
    inv_hd = 1.0 / RWKV_HD
    mean = _headsum(y, ones_bd) * inv_hd
    d = y - mean
    var = _headsum(d * d, ones_bd) * inv_hd
    yn = d * lax.rsqrt(var + RWKV_GN_EPS) * gnw_ref[...] + gnb_ref[...]
    yn = yn + _headsum(r * k * rk_ref[...], ones_bd) * v
    y_ref[0] = (yn * _silu(z_ref[0].astype(F32))).astype(BF16)

    @pl.when(li == pl.num_programs(1) - 1)
    def _():
        for g in range(RWKV_NGROUPS):
            for j in range(RWKV_GROUP):
                d = slice(j * RWKV_HD, (j + 1) * RWKV_HD)
                s_ref[0, g * RWKV_GROUP + j] = states[g][d, d]


def _rwkv(proj, sh_rkv, sh_wa, s0, mu_rkv, mu_wa, w0, wup_pad, a0, aup_pad, k_k, k_a, r_k, gn_w, gn_b, C):
    B, L, _ = proj.shape
    TL = min(L, MIXER_TL)
    W = RWKV_WIDTH
    rb, kb, vb, zb = COL_R // W, COL_KRW // W, COL_VRW // W, COL_ZRW // W
    wab = COL_WA // LANE
    vec = pl.BlockSpec((1, W), lambda b, l: (0, 0))
    mat = pl.BlockSpec((LANE, W), lambda b, l: (0, 0))
    tok = lambda cb: pl.BlockSpec((1, TL, W), lambda b, l: (b, l, cb))
    st_spec = pl.BlockSpec((1, RWKV_HEADS, RWKV_HD, RWKV_HD), lambda b, l: (b, 0, 0, 0))
    return pl.pallas_call(
        functools.partial(_rwkv_kernel, C=C),
        grid=(B, L // TL),
        in_specs=[
            tok(rb), tok(kb), tok(vb),
            pl.BlockSpec((1, TL, LANE), lambda b, l: (b, l, wab)),
            tok(zb),
            pl.BlockSpec((1, 1, 3 * W), lambda b, l: (b, 0, 0)),
            pl.BlockSpec((1, 1, LANE), lambda b, l: (b, 0, 0)),
            st_spec,
            pl.BlockSpec((1, 3 * W), lambda b, l: (0, 0)),
            pl.BlockSpec((1, LANE), lambda b, l: (0, 0)),
            vec, mat, vec, mat, vec, vec, vec, vec, vec,
        ],
        out_specs=[
            pl.BlockSpec((1, TL, W), lambda b, l: (b, l, 0)),
            st_spec,
        ],
        out_shape=[
            jax.ShapeDtypeStruct((B, L, W), BF16),
            jax.ShapeDtypeStruct((B, RWKV_HEADS, RWKV_HD, RWKV_HD), F32),
        ],
        scratch_shapes=[
            pltpu.VMEM((3, TL + SUBLANE, W), F32),
            pltpu.VMEM((TL + SUBLANE, LANE), F32),
            pltpu.VMEM((RWKV_NGROUPS, RWKV_GW, RWKV_GW), F32),
        ] + [pltpu.VMEM((TL, W), F32)] * 7 + [
            pltpu.VMEM((TL, W), BF16),
            pltpu.VMEM((TL // C, RWKV_NGROUPS, RWKV_GW, RWKV_GW), BF16),
            pltpu.VMEM((TL // C, RWKV_NGROUPS, RWKV_GW, RWKV_GW), F32),
            pltpu.VMEM((TL // C, 1, W), F32),
        ],
        compiler_params=_cparams(("parallel", "arbitrary")),
        name="rwkv7",
    )(proj, proj, proj, proj, proj, sh_rkv, sh_wa, s0,
      mu_rkv, mu_wa, w0, wup_pad, a0, aup_pad, k_k, k_a, r_k, gn_w, gn_b)


def _outproj_kernel(x_ref, yg_ref, yl_ref, yr_ref, ga_ref, gb_ref, gc_ref,
                    wg_ref, wl_ref, wr_ref, wo_ref, fg_ref, o_ref, *, final):
    merged = (_sigmoid(ga_ref[...].astype(F32)) * _dot(yg_ref[...], wg_ref[...])
              + _sigmoid(gb_ref[...].astype(F32)) * _dot(yl_ref[...], wl_ref[...])
              + _sigmoid(gc_ref[...].astype(F32)) * _dot(yr_ref[...], wr_ref[...]))
    x = x_ref[...] + _dot(merged.astype(BF16), wo_ref[...])
    if final:
        x = x * lax.rsqrt(jnp.mean(x * x, axis=-1, keepdims=True) + NORM_EPS) * fg_ref[...]
    o_ref[...] = x


def _outproj(x2d, yg, yl, yr, proj2d, wg, wl, wr, wo, fg, final):
    T = x2d.shape[0]
    tm = min(T, OUTPROJ_TM)
    D = D_MODEL
    gb = COL_GATES // D
    tok = lambda cb: pl.BlockSpec((tm, D), lambda i: (i, cb))
    wspec = pl.BlockSpec((D, D), lambda i: (0, 0))
    return pl.pallas_call(
        functools.partial(_outproj_kernel, final=final),
        grid=(T // tm,),
        in_specs=[tok(0), tok(0), tok(0), tok(0), tok(gb), tok(gb + 1), tok(gb + 2),
                  wspec, wspec, wspec, wspec, pl.BlockSpec((1, D), lambda i: (0, 0))],
        out_specs=tok(0),
        out_shape=jax.ShapeDtypeStruct((T, D), F32),
        compiler_params=_cparams(("parallel",)),
        name="outproj",
    )(x2d, yg, yl, yr, proj2d, proj2d, proj2d, wg, wl, wr, wo, fg)


def _relayout_kernel(w_ref, o_ref):
    o_gd = GLA_DK + GLA_DK + GLA_DV
    o_zg = o_gd + GLA_RANK
    o_wa = o_zg + GLA_DV + 2 * LRU_WIDTH + 3 * RWKV_WIDTH
    o_zr = o_wa + DECAY_RANK + AAA_RANK
    d_in = w_ref.shape[1]
    pieces = [(COL_Q, 0, o_gd),
              (COL_ZGLA, o_zg, o_wa),
              (COL_ZRW, o_zr, d_in),
              (COL_GD, o_gd, o_zg),
              (COL_WA, o_wa, o_zr)]
    for dst, lo, hi in pieces:
        o_ref[0, dst:dst + hi - lo, :] = w_ref[0, lo:hi, :].astype(BF16)
    o_ref[0, COL_GD + GLA_RANK:COL_WA, :] = jnp.zeros((COL_WA - COL_GD - GLA_RANK, o_ref.shape[2]), BF16)


def _relayout_w_in(w_in):
    wt = jnp.swapaxes(w_in, 1, 2)
    d_in = wt.shape[1]
    return pl.pallas_call(
        _relayout_kernel,
        grid=(DEPTH, D_MODEL // LANE),
        in_specs=[pl.BlockSpec((1, d_in, LANE), lambda l, i: (l, 0, i))],
        out_specs=pl.BlockSpec((1, D_PROJ, LANE), lambda l, i: (l, 0, i)),
        out_shape=jax.ShapeDtypeStruct((DEPTH, D_PROJ, D_MODEL), BF16),
        compiler_params=_cparams(("parallel", "parallel")),
        name="relayout_w_in",
    )(wt)


def _prep_layer(P, l):
    zpad = jnp.zeros((LANE - GLA_RANK, GLA_DK), F32)
    z64 = jnp.zeros((DECAY_RANK, RWKV_WIDTH), F32)
    mu = P['rwkv_mu'][l]
    row = lambda a: a.reshape(1, -1)
    return dict(
        norm_g=row(P['norm_g'][l]),
        wg_pad=jnp.concatenate([P['gla_w_gup'][l], zpad], axis=0).astype(BF16),
        bg=row(P['gla_b_g'][l]), ng=row(P['gla_norm_g'][l]),
        cw=P['lru_conv_w'][l], cb=row(P['lru_conv_b'][l]),
        wax=jnp.concatenate([P['lru_w_a'][l], P['lru_w_x'][l]], axis=-1).astype(BF16),
        ba=row(P['lru_b_a'][l]), bx=row(P['lru_b_x'][l]), lam=row(P['lru_lambda'][l]),
        mu_rkv=row(mu[:3 * RWKV_WIDTH]), mu_wa=row(mu[3 * RWKV_WIDTH:]),
        w0=row(P['rwkv_w0'][l]),
        wup_pad=jnp.concatenate([P['rwkv_w_up'][l], z64], axis=0).astype(BF16),
        a0=row(P['rwkv_a0'][l]),
        aup_pad=jnp.concatenate([z64, P['rwkv_a_up'][l]], axis=0).astype(BF16),
        k_k=row(P['rwkv_k_k'][l]), k_a=row(P['rwkv_k_a'][l]), r_k=row(P['rwkv_r_k'][l]),
        gn_w=row(P['rwkv_gn_w'][l]), gn_b=row(P['rwkv_gn_b'][l]),
        wpg=P['w_proj_gla'][l].astype(BF16), wpl=P['w_proj_lru'][l].astype(BF16),
        wpr=P['w_proj_rwkv'][l].astype(BF16), wo=P['w_out'][l].astype(BF16),
    )


def _to_kernel_states(gla, lru_h, lru_conv, rwkv, shift):
    Bn = gla.shape[1]
    conv_pad = jnp.zeros((DEPTH, Bn, SUBLANE - (CONV_WIDTH - 1), LRU_WIDTH), lru_conv.dtype)
    return (gla, lru_h[:, :, None, :], jnp.concatenate([conv_pad, lru_conv], axis=2), rwkv,
            shift[:, :, None, :3 * RWKV_WIDTH], shift[:, :, None, 3 * RWKV_WIDTH:])


def _zero_kernel_states(Bn, dt):
    z = lambda *shape: jnp.zeros((DEPTH, Bn) + shape, dt)
    return (z(GLA_HEADS, GLA_HK, GLA_HV), z(1, LRU_WIDTH), z(SUBLANE, LRU_WIDTH),
            z(RWKV_HEADS, RWKV_HD, RWKV_HD), z(1, 3 * RWKV_WIDTH), z(1, LANE))


def _run_trunk(x, states, layers, w_all, final_g, C):
    gla0, lru_h0, conv0_pad, rwkv0, sh_rkv, sh_wa = states
    B, L, D = x.shape
    T = B * L
    x2d = x.reshape(T, D)
    n_gla, n_h, n_conv, n_rw, n_shift = [], [], [], [], []
    for l in range(DEPTH):
        p = layers[l]
        tiles_per_seq = L // INPROJ_TM if L % INPROJ_TM == 0 else 0
        proj2d, tails = _inproj(x2d, p['norm_g'], w_all, l, BF16 if tiles_per_seq else F32)
        proj = proj2d.reshape(B, L, D_PROJ)
        if tiles_per_seq:
            tail = tails.reshape(B, tiles_per_seq, SUBLANE, D_PROJ)[:, -1]
        else:
            tail = proj[:, L - SUBLANE:, :]

        yg, s_gla = _gla(proj, p['wg_pad'], p['bg'], p['ng'], gla0[l], C)
        yl, h_last = _lru(proj, conv0_pad[l], lru_h0[l], p['cw'], p['cb'], p['wax'],
                          p['ba'], p['bx'], p['lam'])
        yr, s_rw = _rwkv(proj, sh_rkv[l], sh_wa[l], rwkv0[l],
                         p['mu_rkv'], p['mu_wa'], p['w0'], p['wup_pad'], p['a0'], p['aup_pad'],
                         p['k_k'], p['k_a'], p['r_k'], p['gn_w'], p['gn_b'], C)

        x2d = _outproj(x2d, yg.reshape(T, D), yl.reshape(T, D), yr.reshape(T, D), proj2d,
                       p['wpg'], p['wpl'], p['wpr'], p['wo'], final_g, final=(l == DEPTH - 1))

        n_gla.append(s_gla)
        n_h.append(h_last)
        n_conv.append(tail[:, SUBLANE - (CONV_WIDTH - 1):, COL_XL:COL_XL + LRU_WIDTH])
        n_rw.append(s_rw)
        n_shift.append(jnp.concatenate(
            [tail[:, -1, COL_R:COL_R + 3 * RWKV_WIDTH], tail[:, -1, COL_WA:COL_WA + 2 * DECAY_RANK]],
            axis=-1))
    return (x2d.reshape(B, L, D), jnp.stack(n_gla), jnp.stack(n_h)[:, :, 0, :],
            jnp.stack(n_conv), jnp.stack(n_rw), jnp.stack(n_shift))


def kernel(x_prompt, x_sample, state_gla, state_lru_h, state_lru_conv, state_rwkv, state_rwkv_shift,
           norm_g, w_in, gla_w_gup, gla_b_g, gla_norm_g,
           lru_conv_w, lru_conv_b, lru_w_a, lru_b_a, lru_w_x, lru_b_x, lru_lambda,
           rwkv_mu, rwkv_w0, rwkv_w_up, rwkv_a0, rwkv_a_up, rwkv_k_k, rwkv_k_a, rwkv_r_k,
           rwkv_gn_w, rwkv_gn_b, w_proj_gla, w_proj_lru, w_proj_rwkv, w_out, final_norm_g):
    P = dict(norm_g=norm_g, w_in=w_in, gla_w_gup=gla_w_gup, gla_b_g=gla_b_g, gla_norm_g=gla_norm_g,
             lru_conv_w=lru_conv_w, lru_conv_b=lru_conv_b, lru_w_a=lru_w_a, lru_b_a=lru_b_a,
             lru_w_x=lru_w_x, lru_b_x=lru_b_x, lru_lambda=lru_lambda,
             rwkv_mu=rwkv_mu, rwkv_w0=rwkv_w0, rwkv_w_up=rwkv_w_up, rwkv_a0=rwkv_a0,
             rwkv_a_up=rwkv_a_up, rwkv_k_k=rwkv_k_k, rwkv_k_a=rwkv_k_a, rwkv_r_k=rwkv_r_k,
             rwkv_gn_w=rwkv_gn_w, rwkv_gn_b=rwkv_gn_b, w_proj_gla=w_proj_gla, w_proj_lru=w_proj_lru,
             w_proj_rwkv=w_proj_rwkv, w_out=w_out)
    layers = [_prep_layer(P, l) for l in range(DEPTH)]
    w_all = _relayout_w_in(w_in)
    final_g = final_norm_g.reshape(1, -1)
    out_p = _run_trunk(x_prompt, _zero_kernel_states(x_prompt.shape[0], x_prompt.dtype),
                       layers, w_all, final_g, C=CHUNK)
    Ls = x_sample.shape[1]
    out_s = _run_trunk(x_sample, _to_kernel_states(state_gla, state_lru_h, state_lru_conv,
                                                   state_rwkv, state_rwkv_shift),
                       layers, w_all, final_g, C=CHUNK if Ls % CHUNK == 0 else Ls)
    return (out_p[0], out_s[0]) + tuple(out_p[1:]) + tuple(out_s[1:])
```

```python
import functools

import numpy as np
import jax
import jax.numpy as jnp
from jax import lax
from jax.experimental import pallas as pl
from jax.experimental.pallas import tpu as pltpu

F32 = jnp.float32
BF16 = jnp.bfloat16

D_MODEL = 1024
DEPTH = 4
CHUNK = 64
NORM_EPS = 1e-6
GLA_HEADS = 4
GLA_HK = 128
GLA_HV = 256
GLA_DK = GLA_HEADS * GLA_HK
GLA_DV = GLA_HEADS * GLA_HV
GLA_RANK = 16
GLA_TAU = 16.0
GLA_HEADS_PER_STEP = 4
LRU_WIDTH = 1024
LRU_BLOCKS = 8
LRU_BS = 128
CONV_WIDTH = 4
LRU_C = 8.0
RWKV_WIDTH = 1024
RWKV_HD = 64
RWKV_HEADS = 16
RWKV_GROUP = 2
RWKV_GW = RWKV_GROUP * RWKV_HD
RWKV_NGROUPS = RWKV_HEADS // RWKV_GROUP
RWKV_LOCAL_CHUNKS = 2
DECAY_RANK = 64
AAA_RANK = 64
RWKV_GN_EPS = 64e-5
RWKV_DECAY_SCALE = 0.6065306597126334
SHIFT_WIDTH = 3 * RWKV_WIDTH + DECAY_RANK + AAA_RANK

LANE = 128
SUBLANE = 8

COL_Q = 0
COL_K = 512
COL_V = 1024
COL_ZGLA = 2048
COL_XL = 3072
COL_ZLRU = 4096
COL_R = 5120
COL_KRW = 6144
COL_VRW = 7168
COL_ZRW = 8192
COL_GATES = 9216
COL_GD = 12288
COL_WA = 12416
D_PROJ = 12544
PROJ_TN = 1792
INPROJ_TM = 1024
OUTPROJ_TM = 512
MIXER_TL = 256
GLA_TL = 512

VMEM_LIMIT = 56 * 1024 * 1024


def _cparams(sem):
    return pltpu.CompilerParams(dimension_semantics=sem, vmem_limit_bytes=VMEM_LIMIT)


def _dot(a, b):
    return jnp.dot(a, b, preferred_element_type=F32)


def _dot_nt(a, b):
    return lax.dot_general(a, b, (((1,), (1,)), ((), ())), preferred_element_type=F32)


def _dot_tn(a, b):
    return lax.dot_general(a, b, (((0,), (0,)), ((), ())), preferred_element_type=F32)


def _sigmoid(x):
    return 1.0 / (1.0 + jnp.exp(-x))


def _silu(x):
    return x * _sigmoid(x)


def _softplus(x):
    return jnp.maximum(x, 0.0) + jnp.log1p(jnp.exp(-jnp.abs(x)))


def _iota2(shape, dim):
    return lax.broadcasted_iota(jnp.int32, shape, dim)


def _cumsum_rows(sel3, x):
    hi = x.astype(BF16)
    r1 = x - hi.astype(F32)
    mid = r1.astype(BF16)
    lo = (r1 - mid.astype(F32)).astype(BF16)
    return _dot(sel3, jnp.concatenate([hi, mid, lo], axis=0))


def _inproj_kernel(x_ref, g_ref, w_ref, o_ref, tail_ref, xn_ref):
    @pl.when(pl.program_id(1) == 0)
    def _():
        x = x_ref[...]
        y = x * lax.rsqrt(jnp.mean(x * x, axis=-1, keepdims=True) + NORM_EPS)
        xn_ref[...] = (y * g_ref[...]).astype(BF16)

    acc = _dot_nt(xn_ref[...], w_ref[...])
    o_ref[...] = acc.astype(o_ref.dtype)
    tail_ref[...] = acc[acc.shape[0] - SUBLANE:, :]


def _inproj(x2d, g, w_all, layer, out_dtype):
    T = x2d.shape[0]
    tm = min(T, INPROJ_TM)
    return pl.pallas_call(
        _inproj_kernel,
        grid=(T // tm, D_PROJ // PROJ_TN),
        in_specs=[
            pl.BlockSpec((tm, D_MODEL), lambda i, j: (i, 0)),
            pl.BlockSpec((1, D_MODEL), lambda i, j: (0, 0)),
            pl.BlockSpec((None, PROJ_TN, D_MODEL), lambda i, j: (layer, j, 0)),
        ],
        out_specs=[
            pl.BlockSpec((tm, PROJ_TN), lambda i, j: (i, j)),
            pl.BlockSpec((SUBLANE, PROJ_TN), lambda i, j: (i, j)),
        ],
        out_shape=[
            jax.ShapeDtypeStruct((T, D_PROJ), out_dtype),
            jax.ShapeDtypeStruct((T // tm * SUBLANE, D_PROJ), F32),
        ],
        scratch_shapes=[pltpu.VMEM((tm, D_MODEL), BF16)],
        compiler_params=_cparams(("parallel", "arbitrary")),
        name="inproj",
    )(x2d, g, w_all)


def _gla_levels(C):
    out, s = [], C // 2
    while s >= 1:
        out.append(s)
        s //= 2
    return out


def _gla_select_matrix(C):
    r = np.arange(C)
    tril = (r[:, None] >= r[None, :])
    blocks = [tril]
    for s in _gla_levels(C):
        boundary = (r // (2 * s)) * (2 * s) + s - 1
        blocks.append(tril[boundary])
    sel = np.concatenate(blocks, axis=0)
    return jnp.asarray(np.concatenate([sel, sel, sel], axis=1), dtype=BF16)


def _gla_kernel(q_ref, k_ref, v_ref, z_ref, gd_ref, wg_ref, bg_ref, ng_ref, s0_ref, gsel_ref,
                y_ref, s_ref, st_ref, *, C):
    li = pl.program_id(2)
    HG = GLA_HEADS_PER_STEP

    @pl.when(li == 0)
    def _():
        for h in range(HG):
            st_ref[h] = s0_ref[0, h].T

    row = _iota2((C, C), 0)
    col = _iota2((C, C), 1)
    eye = row == col
    masks = [((row // (2 * s)) == (col // (2 * s))) & ((row % (2 * s)) >= s) & ((col % (2 * s)) < s)
             for s in _gla_levels(C)]
    gsel = gsel_ref[...]
    TL = q_ref.shape[1]
    probs = [(h, slice(c * C, (c + 1) * C)) for h in range(HG) for c in range(TL // C)]
    kcols = lambda h: slice(h * GLA_HK, (h + 1) * GLA_HK)
    vcols = lambda h: slice(h * GLA_HV, (h + 1) * GLA_HV)

    def stage_decay(prob):
        h, sl = prob
        x = _dot(gd_ref[0, sl, :].astype(BF16), wg_ref[:, kcols(h)]) + bg_ref[:, kcols(h)]
        la = (jnp.minimum(x, 0.0) - jnp.log1p(jnp.exp(-jnp.abs(x)))) * (1.0 / GLA_TAU)
        return dict(h=h, sl=sl, d_all=_cumsum_rows(gsel, la))

    def stage_scores(p):
        h, sl = p['h'], p['sl']
        q = q_ref[0, sl, kcols(h)].astype(F32) * (GLA_HK ** -0.5)
        k = k_ref[0, sl, kcols(h)].astype(F32)
        d_all = p['d_all']
        b = d_all[:C]
        bend = b[C - 1:C, :]
        a = jnp.where(eye, jnp.sum(q * k, axis=-1, keepdims=True), 0.0)
        for lvl, mask in enumerate(masks):
            d = b - d_all[(lvl + 1) * C:(lvl + 2) * C]
            qs = q * jnp.exp(jnp.minimum(d, 0.0))
            ks = k * jnp.exp(jnp.minimum(-d, 0.0))
            a = jnp.where(mask, _dot_nt(qs.astype(BF16), ks.astype(BF16)), a)
        return dict(h=h, sl=sl, a=a.astype(BF16), qd=(q * jnp.exp(b)).astype(BF16),
                    kd=(k * jnp.exp(bend - b)).astype(BF16), wend=jnp.exp(bend))

    def stage_values(p):
        v = v_ref[0, p['sl'], vcols(p['h'])].astype(BF16)
        p.update(o=_dot(p['a'], v), upd=_dot_tn(v, p['kd']))
        return p

    ps = [stage_decay(prob) for prob in probs]
    ps = [stage_scores(p) for p in ps]
    ps = [stage_values(p) for p in ps]

    sts = [st_ref[h] for h in range(HG)]
    for p in ps:
        h, sl = p['h'], p['sl']
        o = p['o'] + _dot_nt(p['qd'], sts[h].astype(BF16))
        sts[h] = sts[h] * p['wend'] + p['upd']
        o = o * lax.rsqrt(jnp.mean(o * o, axis=-1, keepdims=True) + NORM_EPS) * ng_ref[:, vcols(h)]
        y_ref[0, sl, vcols(h)] = (o * _silu(z_ref[0, sl, vcols(h)].astype(F32))).astype(BF16)
    for h in range(HG):
        st_ref[h] = sts[h]

    @pl.when(li == pl.num_programs(2) - 1)
    def _():
        for h in range(HG):
            s_ref[0, h] = sts[h].T


def _gla(proj, wg_pad, bg, ng, s0, C):
    B, L, _ = proj.shape
    TL = min(L, GLA_TL)
    HG = GLA_HEADS_PER_STEP
    KW, VW = HG * GLA_HK, HG * GLA_HV
    qb, kb = COL_Q // KW, COL_K // KW
    vb, zb = COL_V // VW, COL_ZGLA // VW
    gb = COL_GD // LANE
    gsel = _gla_select_matrix(C)
    st_spec = pl.BlockSpec((1, HG, GLA_HK, GLA_HV), lambda b, h, l: (b, h, 0, 0))
    return pl.pallas_call(
        functools.partial(_gla_kernel, C=C),
        grid=(B, GLA_HEADS // HG, L // TL),
        in_specs=[
            pl.BlockSpec((1, TL, KW), lambda b, h, l: (b, l, qb + h)),
            pl.BlockSpec((1, TL, KW), lambda b, h, l: (b, l, kb + h)),
            pl.BlockSpec((1, TL, VW), lambda b, h, l: (b, l, vb + h)),
            pl.BlockSpec((1, TL, VW), lambda b, h, l: (b, l, zb + h)),
            pl.BlockSpec((1, TL, LANE), lambda b, h, l: (b, l, gb)),
            pl.BlockSpec((LANE, KW), lambda b, h, l: (0, h)),
            pl.BlockSpec((1, KW), lambda b, h, l: (0, h)),
            pl.BlockSpec((1, VW), lambda b, h, l: (0, h)),
            st_spec,
            pl.BlockSpec(gsel.shape, lambda b, h, l: (0, 0)),
        ],
        out_specs=[
            pl.BlockSpec((1, TL, VW), lambda b, h, l: (b, l, h)),
            st_spec,
        ],
        out_shape=[
            jax.ShapeDtypeStruct((B, L, GLA_DV), BF16),
            jax.ShapeDtypeStruct((B, GLA_HEADS, GLA_HK, GLA_HV), F32),
        ],
        scratch_shapes=[pltpu.VMEM((HG, GLA_HV, GLA_HK), F32)],
        compiler_params=_cparams(("parallel", "parallel", "arbitrary")),
        name="gla",
    )(proj, proj, proj, proj, proj, wg_pad, bg, ng, s0, gsel)


def _lru_kernel(x_ref, z_ref, c0_ref, h0_ref, cw_ref, cb_ref, wax_ref, ba_ref, bx_ref, lam_ref,
                y_ref, hl_ref, xpad_ref, a_ref, u_ref, hs_ref, h_ref):
    li = pl.program_id(1)
    TL = x_ref.shape[1]
    PAD = SUBLANE

    @pl.when(li == 0)
    def _():
        xpad_ref[0:PAD, :] = c0_ref[0]
        h_ref[...] = h0_ref[0]

    xpad_ref[PAD:PAD + TL, :] = x_ref[0].astype(F32)
    xp = xpad_ref[...]
    acc = xp * cw_ref[0:1, :]
    for t in range(1, CONV_WIDTH):
        acc = pltpu.roll(acc, 1, axis=0) + xp * cw_ref[t:t + 1, :]
    xc = acc[PAD:, :] + cb_ref[...]
    xc_b = xc.astype(BF16)
    rs, xs = [], []
    for n in range(LRU_BLOCKS):
        g = _dot(xc_b[:, n * LRU_BS:(n + 1) * LRU_BS], wax_ref[n])
        rs.append(g[:, :LRU_BS])
        xs.append(g[:, LRU_BS:])
    r = _sigmoid(jnp.concatenate(rs, axis=1) + ba_ref[...])
    i = _sigmoid(jnp.concatenate(xs, axis=1) + bx_ref[...])
    log_a = (-LRU_C) * r * _softplus(-lam_ref[...])
    a_ref[...] = jnp.exp(log_a)
    th = jnp.tanh(log_a)
    u_ref[...] = jnp.sqrt(-2.0 * th / (1.0 - th)) * (i * xc)

    row8 = _iota2((SUBLANE, LRU_WIDTH), 0)

    def body(g, h):
        base = pl.multiple_of(g * SUBLANE, SUBLANE)
        a = a_ref[pl.ds(base, SUBLANE), :]
        u = u_ref[pl.ds(base, SUBLANE), :]
        s = 1
        while s < SUBLANE:
            keep = row8 >= s
            u = u + a * jnp.where(keep, pltpu.roll(u, s, axis=0), 0.0)
            a = a * jnp.where(keep, pltpu.roll(a, s, axis=0), 1.0)
            s *= 2
        hs = a * h + u
        hs_ref[pl.ds(base, SUBLANE), :] = hs
        return hs[SUBLANE - 1:SUBLANE, :]

    h = lax.fori_loop(0, TL // SUBLANE, body, h_ref[...], unroll=2)
    h_ref[...] = h
    hl_ref[0] = h
    y_ref[0] = (hs_ref[...] * _silu(z_ref[0].astype(F32))).astype(BF16)
    xpad_ref[0:PAD, :] = xpad_ref[TL:TL + PAD, :]


def _lru(proj, conv0_pad, h0, cw, cb, wax, ba, bx, lam):
    B, L, _ = proj.shape
    TL = min(L, MIXER_TL)
    W = LRU_WIDTH
    xb, zb = COL_XL // W, COL_ZLRU // W
    vec = pl.BlockSpec((1, W), lambda b, l: (0, 0))
    return pl.pallas_call(
        _lru_kernel,
        grid=(B, L // TL),
        in_specs=[
            pl.BlockSpec((1, TL, W), lambda b, l: (b, l, xb)),
            pl.BlockSpec((1, TL, W), lambda b, l: (b, l, zb)),
            pl.BlockSpec((1, SUBLANE, W), lambda b, l: (b, 0, 0)),
            pl.BlockSpec((1, 1, W), lambda b, l: (b, 0, 0)),
            pl.BlockSpec((CONV_WIDTH, W), lambda b, l: (0, 0)),
            vec,
            pl.BlockSpec((LRU_BLOCKS, LRU_BS, 2 * LRU_BS), lambda b, l: (0, 0, 0)),
            vec, vec, vec,
        ],
        out_specs=[
            pl.BlockSpec((1, TL, W), lambda b, l: (b, l, 0)),
            pl.BlockSpec((1, 1, W), lambda b, l: (b, 0, 0)),
        ],
        out_shape=[
            jax.ShapeDtypeStruct((B, L, W), BF16),
            jax.ShapeDtypeStruct((B, 1, W), F32),
        ],
        scratch_shapes=[
            pltpu.VMEM((TL + SUBLANE, W), F32),
            pltpu.VMEM((TL, W), F32),
            pltpu.VMEM((TL, W), F32),
            pltpu.VMEM((TL, W), F32),
            pltpu.VMEM((1, W), F32),
        ],
        compiler_params=_cparams(("parallel", "arbitrary")),
        name="rglru",
    )(proj, proj, conv0_pad, h0, cw, cb, wax, ba, bx, lam)


def _headsum(x, ones_bd):
    xb = x.astype(BF16)
    outs = [_dot(xb[:, g * RWKV_GW:(g + 1) * RWKV_GW], ones_bd) for g in range(RWKV_NGROUPS)]
    return jnp.concatenate(outs, axis=1)


def _rwkv_local(loads, C, masks):
    tril_b, strict_rb, incl_rb, eye_rb, bd_sq, bd_cv, bd_state = masks

    def tile_rows(x, n):
        return jnp.concatenate([x] * n, axis=0)

    def blockdiag_cv(x):
        return tile_rows(x.astype(BF16), RWKV_GROUP) * bd_cv

    def blockdiag_sq(x):
        return tile_rows(x.astype(BF16), RWKV_GROUP) * bd_sq

    def stage_decay(load):
        r, k, v, kk, a, lw = load()
        lc = _cumsum_rows(tril_b, lw)
        wend = jnp.exp(lc[C - 1:C, :])
        einv = jnp.exp(-lc)
        bt = kk * jnp.exp(lc - lw)
        rt = r * jnp.exp(lc)
        at = -(kk * a) * einv
        kt = k * einv
        return dict(v=v, wend=wend, bt=bt, rt=rt, at=at, kt=kt)

    def stage_scores(p):
        br = jnp.concatenate([p['bt'], p['rt']], axis=0).astype(BF16)
        pa = _dot_nt(br, blockdiag_cv(p['at']))
        pk = _dot_nt(br, blockdiag_cv(p['kt']))
        a_ba = jnp.where(strict_rb, pa[:C], 0.0)
        p.update(a_ra=jnp.where(incl_rb, pa[C:], 0.0).astype(BF16),
                 a_bk=jnp.where(strict_rb, pk[:C], 0.0).astype(BF16),
                 a_rk=jnp.where(incl_rb, pk[C:], 0.0).astype(BF16),
                 x=a_ba, xb=blockdiag_sq(a_ba), t=eye_rb + a_ba)
        return p

    def stage_values(p):
        xv = _dot(jnp.concatenate([p['a_bk'], p['a_rk']], axis=0), blockdiag_cv(p['v']))
        p.update(xv=xv[:C], yv=xv[C:])
        return p

    def stage_square(p):
        p['x'] = _dot(p['x'].astype(BF16), p['xb'])
        p['xb'] = blockdiag_sq(p['x'])
        return p

    def stage_double(p, last):
        if last:
            p['t'] = p['t'] + _dot(p['t'].astype(BF16), p['xb'])
            return p
        xt = _dot(jnp.concatenate([p['x'], p['t']], axis=0).astype(BF16), p['xb'])
        p['x'] = xt[:C]
        p['t'] = p['t'] + xt[C:]
        p['xb'] = blockdiag_sq(p['x'])
        return p

    def stage_apply(p):
        tb = p['t'].astype(BF16)
        p.update(bhat=_dot(tb, blockdiag_cv(p['bt'])), uloc=_dot(tb, blockdiag_cv(p['xv'])))
        return p

    def stage_out(p):
        rhat = p['rt'] + _dot(p['a_ra'], blockdiag_cv(p['bhat']))
        yloc = _dot(p['a_ra'], blockdiag_cv(p['uloc'])) + p['yv']
        atw = (p['at'] * p['wend']).astype(BF16)
        ktw = (p['kt'] * p['wend']).astype(BF16)
        ghat = jnp.where(bd_state, _dot_tn(p['bhat'].astype(BF16), atw), 0.0)
        h = jnp.where(bd_state,
                      _dot_tn(jnp.concatenate([p['uloc'], p['v']], axis=0).astype(BF16),
                              jnp.concatenate([atw, ktw], axis=0)), 0.0)
        return rhat, yloc, ghat, h, p['wend']

    ps = [stage_decay(ld) for ld in loads]
    ps = [stage_scores(p) for p in ps]
    ps = [stage_values(p) for p in ps]
    ps = [stage_square(p) for p in ps]
    n = 4
    while n <= C:
        ps = [stage_double(p, last=(n == C)) for p in ps]
        n *= 2
    ps = [stage_apply(p) for p in ps]
    return [stage_out(p) for p in ps]


def _rwkv_masks(C):
    G = RWKV_GROUP
    tril = (_iota2((C, 3 * C), 0) >= (_iota2((C, 3 * C), 1) % C)).astype(BF16)
    r_rb = _iota2((C, G * C), 0)
    c_rb = _iota2((C, G * C), 1) % C
    strict_rb = r_rb > c_rb
    incl_rb = r_rb >= c_rb
    eye_rb = (r_rb == c_rb).astype(F32)
    bd_sq = ((_iota2((G * C, G * C), 0) // C) == (_iota2((G * C, G * C), 1) // C)).astype(BF16)
    bd_cv = ((_iota2((G * C, RWKV_GW), 0) // C) == (_iota2((G * C, RWKV_GW), 1) // RWKV_HD)).astype(BF16)
    bd_state = (_iota2((RWKV_GW, RWKV_GW), 0) // RWKV_HD) == (_iota2((RWKV_GW, RWKV_GW), 1) // RWKV_HD)
    return tril, strict_rb, incl_rb, eye_rb, bd_sq, bd_cv, bd_state


def _rwkv_kernel(r_ref, k_ref, v_ref, wa_ref, z_ref, sh_rkv_ref, sh_wa_ref, s0_ref,
                 mu_rkv_ref, mu_wa_ref, w0_ref, wup_ref, a0_ref, aup_ref, kk_ref, ka_ref, rk_ref,
                 gnw_ref, gnb_ref,
                 y_ref, s_ref,
                 xpad_ref, wapad_ref, st_ref, rs_ref, ks_ref, vs_ref, kks_ref, as_ref, lws_ref, ys_ref,
                 rhat_ref, ghat_ref, h_ref, wend_ref, *, C):
    li = pl.program_id(1)
    TL = r_ref.shape[1]
    PAD = SUBLANE
    W = RWKV_WIDTH

    @pl.when(li == 0)
    def _():
        for n in range(3):
            xpad_ref[n, PAD - 1:PAD, :] = sh_rkv_ref[0, :, n * W:(n + 1) * W]
        wapad_ref[PAD - 1:PAD, :] = sh_wa_ref[0]
        zero = jnp.zeros((RWKV_HD, RWKV_HD), F32)
        for g in range(RWKV_NGROUPS):
            st_ref[g] = jnp.concatenate(
                [jnp.concatenate([s0_ref[0, g * RWKV_GROUP + i] if i == j else zero
                                  for i in range(RWKV_GROUP)], axis=1)
                 for j in range(RWKV_GROUP)], axis=0)

    mixed = []
    for n, ref in enumerate((r_ref, k_ref, v_ref)):
        x = ref[0].astype(F32)
        xpad_ref[n, PAD:PAD + TL, :] = x
        prev = xpad_ref[n, PAD - 1:PAD - 1 + TL, :]
        mixed.append(x + (prev - x) * mu_rkv_ref[:, n * W:(n + 1) * W])
        xpad_ref[n, PAD - 1:PAD, :] = xpad_ref[n, PAD + TL - 1:PAD + TL, :]
    r, k, v = mixed
    xwa = wa_ref[0].astype(F32)
    wapad_ref[PAD:PAD + TL, :] = xwa
    prev = wapad_ref[PAD - 1:PAD - 1 + TL, :]
    wapad_ref[PAD - 1:PAD, :] = wapad_ref[PAD + TL - 1:PAD + TL, :]
    xwa = xwa + (prev - xwa) * mu_wa_ref[...]

    w = w0_ref[...] + _dot(jnp.tanh(xwa).astype(BF16), wup_ref[...])
    lw = (-RWKV_DECAY_SCALE) * _sigmoid(w)
    a = _sigmoid(a0_ref[...] + _dot(xwa.astype(BF16), aup_ref[...]))

    ones_bd = ((_iota2((RWKV_GW, RWKV_GW), 0) // RWKV_HD)
               == (_iota2((RWKV_GW, RWKV_GW), 1) // RWKV_HD)).astype(BF16)
    kk = k * kk_ref[...]
    kk = kk * lax.rsqrt(_headsum(kk * kk, ones_bd) + 1e-12)
    k = k * (1.0 + (a - 1.0) * ka_ref[...])
    rs_ref[...] = r
    ks_ref[...] = k
    vs_ref[...] = v
    kks_ref[...] = kk
    as_ref[...] = a
    lws_ref[...] = lw

    masks = _rwkv_masks(C)

    NCH = TL // C
    NB = min(NCH, RWKV_LOCAL_CHUNKS)

    def loader(rows, cols):
        return lambda: tuple(ref[rows, cols]
                             for ref in (rs_ref, ks_ref, vs_ref, kks_ref, as_ref, lws_ref))

    for cb in range(NCH // NB):
        probs = []
        for c in range(cb * NB, (cb + 1) * NB):
            rows = slice(c * C, (c + 1) * C)
            for g in range(RWKV_NGROUPS):
                probs.append((c, rows, g, slice(g * RWKV_GW, (g + 1) * RWKV_GW)))
        outs = _rwkv_local([loader(rows, cols) for _, rows, _, cols in probs], C, masks)
        for (c, rows, g, cols), (rhat, yloc, ghat, h, wend) in zip(probs, outs):
            rhat_ref[rows, cols] = rhat.astype(BF16)
            ys_ref[rows, cols] = yloc
            ghat_ref[c, g] = ghat.astype(BF16)
            h_ref[c, g] = h
            wend_ref[c, :, cols] = wend

    states = [st_ref[g] for g in range(RWKV_NGROUPS)]
    for c in range(NCH):
        rows = slice(c * C, (c + 1) * C)
        for g in range(RWKV_NGROUPS):
            cols = slice(g * RWKV_GW, (g + 1) * RWKV_GW)
            sb = states[g].astype(BF16)
            ys_ref[rows, cols] = ys_ref[rows, cols] + _dot_nt(rhat_ref[rows, cols], sb)
            states[g] = states[g] * wend_ref[c, :, cols] + _dot(sb, ghat_ref[c, g]) + h_ref[c, g]
    for g in range(RWKV_NGROUPS):
        st_ref[g] = states[g]

    y = ys_ref[...]
    r = rs_ref[...]
    k = ks_ref[...]
    v = vs_ref[...]
    inv_hd = 1.0 / RWKV_HD
    mean = _headsum(y, ones_bd) * inv_hd
    d = y - mean
    var = _headsum(d * d, ones_bd) * inv_hd
    yn = d * lax.rsqrt(var + RWKV_GN_EPS) * gnw_ref[...] + gnb_ref[...]
    yn = yn + _headsum(r * k * rk_ref[...], ones_bd) * v
    y_ref[0] = (yn * _silu(z_ref[0].astype(F32))).astype(BF16)

    @pl.when(li == pl.num_programs(1) - 1)
    def _():
        for g in range(RWKV_NGROUPS):
            for j in range(RWKV_GROUP):
                d = slice(j * RWKV_HD, (j + 1) * RWKV_HD)
                s_ref[0, g * RWKV_GROUP + j] = states[g][d, d]


def _rwkv(proj, sh_rkv, sh_wa, s0, mu_rkv, mu_wa, w0, wup_pad, a0, aup_pad, k_k, k_a, r_k, gn_w, gn_b, C):
    B, L, _ = proj.shape
    TL = min(L, MIXER_TL)
    W = RWKV_WIDTH
    rb, kb, vb, zb = COL_R // W, COL_KRW // W, COL_VRW // W, COL_ZRW // W
    wab = COL_WA // LANE
    vec = pl.BlockSpec((1, W), lambda b, l: (0, 0))
    mat = pl.BlockSpec((LANE, W), lambda b, l: (0, 0))
    tok = lambda cb: pl.BlockSpec((1, TL, W), lambda b, l: (b, l, cb))
    st_spec = pl.BlockSpec((1, RWKV_HEADS, RWKV_HD, RWKV_HD), lambda b, l: (b, 0, 0, 0))
    return pl.pallas_call(
        functools.partial(_rwkv_kernel, C=C),
        grid=(B, L // TL),
        in_specs=[
            tok(rb), tok(kb), tok(vb),
            pl.BlockSpec((1, TL, LANE), lambda b, l: (b, l, wab)),
            tok(zb),
            pl.BlockSpec((1, 1, 3 * W), lambda b, l: (b, 0, 0)),
            pl.BlockSpec((1, 1, LANE), lambda b, l: (b, 0, 0)),
            st_spec,
            pl.BlockSpec((1, 3 * W), lambda b, l: (0, 0)),
            pl.BlockSpec((1, LANE), lambda b, l: (0, 0)),
            vec, mat, vec, mat, vec, vec, vec, vec, vec,
        ],
        out_specs=[
            pl.BlockSpec((1, TL, W), lambda b, l: (b, l, 0)),
            st_spec,
        ],
        out_shape=[
            jax.ShapeDtypeStruct((B, L, W), BF16),
            jax.ShapeDtypeStruct((B, RWKV_HEADS, RWKV_HD, RWKV_HD), F32),
        ],
        scratch_shapes=[
            pltpu.VMEM((3, TL + SUBLANE, W), F32),
            pltpu.VMEM((TL + SUBLANE, LANE), F32),
            pltpu.VMEM((RWKV_NGROUPS, RWKV_GW, RWKV_GW), F32),
        ] + [pltpu.VMEM((TL, W), F32)] * 7 + [
            pltpu.VMEM((TL, W), BF16),
            pltpu.VMEM((TL // C, RWKV_NGROUPS, RWKV_GW, RWKV_GW), BF16),
            pltpu.VMEM((TL // C, RWKV_NGROUPS, RWKV_GW, RWKV_GW), F32),
            pltpu.VMEM((TL // C, 1, W), F32),
        ],
        compiler_params=_cparams(("parallel", "arbitrary")),
        name="rwkv7",
    )(proj, proj, proj, proj, proj, sh_rkv, sh_wa, s0,
      mu_rkv, mu_wa, w0, wup_pad, a0, aup_pad, k_k, k_a, r_k, gn_w, gn_b)


def _outproj_kernel(x_ref, yg_ref, yl_ref, yr_ref, ga_ref, gb_ref, gc_ref,
                    wg_ref, wl_ref, wr_ref, wo_ref, fg_ref, o_ref, *, final):
    merged = (_sigmoid(ga_ref[...].astype(F32)) * _dot(yg_ref[...], wg_ref[...])
              + _sigmoid(gb_ref[...].astype(F32)) * _dot(yl_ref[...], wl_ref[...])
              + _sigmoid(gc_ref[...].astype(F32)) * _dot(yr_ref[...], wr_ref[...]))
    x = x_ref[...] + _dot(merged.astype(BF16), wo_ref[...])
    if final:
        x = x * lax.rsqrt(jnp.mean(x * x, axis=-1, keepdims=True) + NORM_EPS) * fg_ref[...]
    o_ref[...] = x


def _outproj(x2d, yg, yl, yr, proj2d, wg, wl, wr, wo, fg, final):
    T = x2d.shape[0]
    tm = min(T, OUTPROJ_TM)
    D = D_MODEL
    gb = COL_GATES // D
    tok = lambda cb: pl.BlockSpec((tm, D), lambda i: (i, cb))
    wspec = pl.BlockSpec((D, D), lambda i: (0, 0))
    return pl.pallas_call(
        functools.partial(_outproj_kernel, final=final),
        grid=(T // tm,),
        in_specs=[tok(0), tok(0), tok(0), tok(0), tok(gb), tok(gb + 1), tok(gb + 2),
                  wspec, wspec, wspec, wspec, pl.BlockSpec((1, D), lambda i: (0, 0))],
        out_specs=tok(0),
        out_shape=jax.ShapeDtypeStruct((T, D), F32),
        compiler_params=_cparams(("parallel",)),
        name="outproj",
    )(x2d, yg, yl, yr, proj2d, proj2d, proj2d, wg, wl, wr, wo, fg)


def _relayout_kernel(w_ref, o_ref):
    o_gd = GLA_DK + GLA_DK + GLA_DV
    o_zg = o_gd + GLA_RANK
    o_wa = o_zg + GLA_DV + 2 * LRU_WIDTH + 3 * RWKV_WIDTH
    o_zr = o_wa + DECAY_RANK + AAA_RANK
    d_in = w_ref.shape[1]
    pieces = [(COL_Q, 0, o_gd),
              (COL_ZGLA, o_zg, o_wa),
              (COL_ZRW, o_zr, d_in),
              (COL_GD, o_gd, o_zg),
              (COL_WA, o_wa, o_zr)]
    for dst, lo, hi in pieces:
        o_ref[0, dst:dst + hi - lo, :] = w_ref[0, lo:hi, :].astype(BF16)
    o_ref[0, COL_GD + GLA_RANK:COL_WA, :] = jnp.zeros((COL_WA - COL_GD - GLA_RANK, o_ref.shape[2]), BF16)


def _relayout_w_in(w_in):
    wt = jnp.swapaxes(w_in, 1, 2)
    d_in = wt.shape[1]
    return pl.pallas_call(
        _relayout_kernel,
        grid=(DEPTH, D_MODEL // LANE),
        in_specs=[pl.BlockSpec((1, d_in, LANE), lambda l, i: (l, 0, i))],
        out_specs=pl.BlockSpec((1, D_PROJ, LANE), lambda l, i: (l, 0, i)),
        out_shape=jax.ShapeDtypeStruct((DEPTH, D_PROJ, D_MODEL), BF16),
        compiler_params=_cparams(("parallel", "parallel")),
        name="relayout_w_in",
    )(wt)


def _prep_layer(P, l):
    zpad = jnp.zeros((LANE - GLA_RANK, GLA_DK), F32)
    z64 = jnp.zeros((DECAY_RANK, RWKV_WIDTH), F32)
    mu = P['rwkv_mu'][l]
    row = lambda a: a.reshape(1, -1)
    return dict(
        norm_g=row(P['norm_g'][l]),
        wg_pad=jnp.concatenate([P['gla_w_gup'][l], zpad], axis=0).astype(BF16),
        bg=row(P['gla_b_g'][l]), ng=row(P['gla_norm_g'][l]),
        cw=P['lru_conv_w'][l], cb=row(P['lru_conv_b'][l]),
        wax=jnp.concatenate([P['lru_w_a'][l], P['lru_w_x'][l]], axis=-1).astype(BF16),
        ba=row(P['lru_b_a'][l]), bx=row(P['lru_b_x'][l]), lam=row(P['lru_lambda'][l]),
        mu_rkv=row(mu[:3 * RWKV_WIDTH]), mu_wa=row(mu[3 * RWKV_WIDTH:]),
        w0=row(P['rwkv_w0'][l]),
        wup_pad=jnp.concatenate([P['rwkv_w_up'][l], z64], axis=0).astype(BF16),
        a0=row(P['rwkv_a0'][l]),
        aup_pad=jnp.concatenate([z64, P['rwkv_a_up'][l]], axis=0).astype(BF16),
        k_k=row(P['rwkv_k_k'][l]), k_a=row(P['rwkv_k_a'][l]), r_k=row(P['rwkv_r_k'][l]),
        gn_w=row(P['rwkv_gn_w'][l]), gn_b=row(P['rwkv_gn_b'][l]),
        wpg=P['w_proj_gla'][l].astype(BF16), wpl=P['w_proj_lru'][l].astype(BF16),
        wpr=P['w_proj_rwkv'][l].astype(BF16), wo=P['w_out'][l].astype(BF16),
    )


def _to_kernel_states(gla, lru_h, lru_conv, rwkv, shift):
    Bn = gla.shape[1]
    conv_pad = jnp.zeros((DEPTH, Bn, SUBLANE - (CONV_WIDTH - 1), LRU_WIDTH), lru_conv.dtype)
    return (gla, lru_h[:, :, None, :], jnp.concatenate([conv_pad, lru_conv], axis=2), rwkv,
            shift[:, :, None, :3 * RWKV_WIDTH], shift[:, :, None, 3 * RWKV_WIDTH:])


def _zero_kernel_states(Bn, dt):
    z = lambda *shape: jnp.zeros((DEPTH, Bn) + shape, dt)
    return (z(GLA_HEADS, GLA_HK, GLA_HV), z(1, LRU_WIDTH), z(SUBLANE, LRU_WIDTH),
            z(RWKV_HEADS, RWKV_HD, RWKV_HD), z(1, 3 * RWKV_WIDTH), z(1, LANE))


def _run_trunk(x, states, layers, w_all, final_g, C):
    gla0, lru_h0, conv0_pad, rwkv0, sh_rkv, sh_wa = states
    B, L, D = x.shape
    T = B * L
    x2d = x.reshape(T, D)
    n_gla, n_h, n_conv, n_rw, n_shift = [], [], [], [], []
    for l in range(DEPTH):
        p = layers[l]
        tiles_per_seq = L // INPROJ_TM if L % INPROJ_TM == 0 else 0
        proj2d, tails = _inproj(x2d, p['norm_g'], w_all, l, BF16 if tiles_per_seq else F32)
        proj = proj2d.reshape(B, L, D_PROJ)
        if tiles_per_seq:
            tail = tails.reshape(B, tiles_per_seq, SUBLANE, D_PROJ)[:, -1]
        else:
            tail = proj[:, L - SUBLANE:, :]

        yg, s_gla = _gla(proj, p['wg_pad'], p['bg'], p['ng'], gla0[l], C)
        yl, h_last = _lru(proj, conv0_pad[l], lru_h0[l], p['cw'], p['cb'], p['wax'],
                          p['ba'], p['bx'], p['lam'])
        yr, s_rw = _rwkv(proj, sh_rkv[l], sh_wa[l], rwkv0[l],
                         p['mu_rkv'], p['mu_wa'], p['w0'], p['wup_pad'], p['a0'], p['aup_pad'],
                         p['k_k'], p['k_a'], p['r_k'], p['gn_w'], p['gn_b'], C)

        x2d = _outproj(x2d, yg.reshape(T, D), yl.reshape(T, D), yr.reshape(T, D), proj2d,
                       p['wpg'], p['wpl'], p['wpr'], p['wo'], final_g, final=(l == DEPTH - 1))

        n_gla.append(s_gla)
        n_h.append(h_last)
        n_conv.append(tail[:, SUBLANE - (CONV_WIDTH - 1):, COL_XL:COL_XL + LRU_WIDTH])
        n_rw.append(s_rw)
        n_shift.append(jnp.concatenate(
            [tail[:, -1, COL_R:COL_R + 3 * RWKV_WIDTH], tail[:, -1, COL_WA:COL_WA + 2 * DECAY_RANK]],
            axis=-1))
    return (x2d.reshape(B, L, D), jnp.stack(n_gla), jnp.stack(n_h)[:, :, 0, :],
            jnp.stack(n_conv), jnp.stack(n_rw), jnp.stack(n_shift))


def kernel(x_prompt, x_sample, state_gla, state_lru_h, state_lru_conv, state_rwkv, state_rwkv_shift,
           norm_g, w_in, gla_w_gup, gla_b_g, gla_norm_g,
           lru_conv_w, lru_conv_b, lru_w_a, lru_b_a, lru_w_x, lru_b_x, lru_lambda,
           rwkv_mu, rwkv_w0, rwkv_w_up, rwkv_a0, rwkv_a_up, rwkv_k_k, rwkv_k_a, rwkv_r_k,
           rwkv_gn_w, rwkv_gn_b, w_proj_gla, w_proj_lru, w_proj_rwkv, w_out, final_norm_g):
    P = dict(norm_g=norm_g, w_in=w_in, gla_w_gup=gla_w_gup, gla_b_g=gla_b_g, gla_norm_g=gla_norm_g,
             lru_conv_w=lru_conv_w, lru_conv_b=lru_conv_b, lru_w_a=lru_w_a, lru_b_a=lru_b_a,
             lru_w_x=lru_w_x, lru_b_x=lru_b_x, lru_lambda=lru_lambda,
             rwkv_mu=rwkv_mu, rwkv_w0=rwkv_w0, rwkv_w_up=rwkv_w_up, rwkv_a0=rwkv_a0,
             rwkv_a_up=rwkv_a_up, rwkv_k_k=rwkv_k_k, rwkv_k_a=rwkv_k_a, rwkv_r_k=rwkv_r_k,
             rwkv_gn_w=rwkv_gn_w, rwkv_gn_b=rwkv_gn_b, w_proj_gla=w_proj_gla, w_proj_lru=w_proj_lru,
             w_proj_rwkv=w_proj_rwkv, w_out=w_out)
    layers = [_prep_layer(P, l) for l in range(DEPTH)]
    w_all = _relayout_w_in(w_in)
    final_g = final_norm_g.reshape(1, -1)
    out_p = _run_trunk(x_prompt, _zero_kernel_states(x_prompt.shape[0], x_prompt.dtype),
                       layers, w_all, final_g, C=CHUNK)
    Ls = x_sample.shape[1]
    out_s = _run_trunk(x_sample, _to_kernel_states(state_gla, state_lru_h, state_lru_conv,
                                                   state_rwkv, state_rwkv_shift),
                       layers, w_all, final_g, C=CHUNK if Ls % CHUNK == 0 else Ls)
    return (out_p[0], out_s[0]) + tuple(out_p[1:]) + tuple(out_s[1:])
```

```python
import functools

import numpy as np
import jax
import jax.numpy as jnp
from jax import lax
from jax.experimental import pallas as pl
from jax.experimental.pallas import tpu as pltpu

F32 = jnp.float32
BF16 = jnp.bfloat16

D_MODEL = 1024
DEPTH = 4
CHUNK = 64
NORM_EPS = 1e-6
GLA_HEADS = 4
GLA_HK = 128
GLA_HV = 256
GLA_DK = GLA_HEADS * GLA_HK
GLA_DV = GLA_HEADS * GLA_HV
GLA_RANK = 16
GLA_TAU = 16.0
GLA_HEADS_PER_STEP = 4
LRU_WIDTH = 1024
LRU_BLOCKS = 8
LRU_BS = 128
CONV_WIDTH = 4
LRU_C = 8.0
RWKV_WIDTH = 1024
RWKV_HD = 64
RWKV_HEADS = 16
RWKV_GROUP = 2
RWKV_GW = RWKV_GROUP * RWKV_HD
RWKV_NGROUPS = RWKV_HEADS // RWKV_GROUP
RWKV_LOCAL_CHUNKS = 2
DECAY_RANK = 64
AAA_RANK = 64
RWKV_GN_EPS = 64e-5
RWKV_DECAY_SCALE = 0.6065306597126334
SHIFT_WIDTH = 3 * RWKV_WIDTH + DECAY_RANK + AAA_RANK

LANE = 128
SUBLANE = 8

COL_Q = 0
COL_K = 512
COL_V = 1024
COL_ZGLA = 2048
COL_XL = 3072
COL_ZLRU = 4096
COL_R = 5120
COL_KRW = 6144
COL_VRW = 7168
COL_ZRW = 8192
COL_GATES = 9216
COL_GD = 12288
COL_WA = 12416
D_PROJ = 12544
PROJ_TN = 1792
INPROJ_TM = 1024
OUTPROJ_TM = 512
MIXER_TL = 256
LONG_TL = 512

VMEM_LIMIT = 56 * 1024 * 1024


def _cparams(sem):
    return pltpu.CompilerParams(dimension_semantics=sem, vmem_limit_bytes=VMEM_LIMIT)


def _dot(a, b):
    return jnp.dot(a, b, preferred_element_type=F32)


def _dot_nt(a, b):
    return lax.dot_general(a, b, (((1,), (1,)), ((), ())), preferred_element_type=F32)


def _dot_tn(a, b):
    return lax.dot_general(a, b, (((0,), (0,)), ((), ())), preferred_element_type=F32)


def _sigmoid(x):
    return 1.0 / (1.0 + jnp.exp(-x))


def _silu(x):
    return x * _sigmoid(x)


def _softplus(x):
    return jnp.maximum(x, 0.0) + jnp.log1p(jnp.exp(-jnp.abs(x)))


def _iota2(shape, dim):
    return lax.broadcasted_iota(jnp.int32, shape, dim)


def _cumsum_rows(sel3, x):
    hi = x.astype(BF16)
    r1 = x - hi.astype(F32)
    mid = r1.astype(BF16)
    lo = (r1 - mid.astype(F32)).astype(BF16)
    return _dot(sel3, jnp.concatenate([hi, mid, lo], axis=0))


def _inproj_kernel(x_ref, g_ref, w_ref, o_ref, tail_ref, xn_ref):
    @pl.when(pl.program_id(1) == 0)
    def _():
        x = x_ref[...]
        y = x * lax.rsqrt(jnp.mean(x * x, axis=-1, keepdims=True) + NORM_EPS)
        xn_ref[...] = (y * g_ref[...]).astype(BF16)

    acc = _dot_nt(xn_ref[...], w_ref[...])
    o_ref[...] = acc.astype(o_ref.dtype)
    tail_ref[...] = acc[acc.shape[0] - SUBLANE:, :]


def _inproj(x2d, g, w_all, layer, out_dtype):
    T = x2d.shape[0]
    tm = min(T, INPROJ_TM)
    return pl.pallas_call(
        _inproj_kernel,
        grid=(T // tm, D_PROJ // PROJ_TN),
        in_specs=[
            pl.BlockSpec((tm, D_MODEL), lambda i, j: (i, 0)),
            pl.BlockSpec((1, D_MODEL), lambda i, j: (0, 0)),
            pl.BlockSpec((None, PROJ_TN, D_MODEL), lambda i, j: (layer, j, 0)),
        ],
        out_specs=[
            pl.BlockSpec((tm, PROJ_TN), lambda i, j: (i, j)),
            pl.BlockSpec((SUBLANE, PROJ_TN), lambda i, j: (i, j)),
        ],
        out_shape=[
            jax.ShapeDtypeStruct((T, D_PROJ), out_dtype),
            jax.ShapeDtypeStruct((T // tm * SUBLANE, D_PROJ), F32),
        ],
        scratch_shapes=[pltpu.VMEM((tm, D_MODEL), BF16)],
        compiler_params=_cparams(("parallel", "arbitrary")),
        name="inproj",
    )(x2d, g, w_all)


def _gla_levels(C):
    out, s = [], C // 2
    while s >= 1:
        out.append(s)
        s //= 2
    return out


def _gla_select_matrix(C):
    r = np.arange(C)
    tril = (r[:, None] >= r[None, :])
    blocks = [tril]
    for s in _gla_levels(C):
        boundary = (r // (2 * s)) * (2 * s) + s - 1
        blocks.append(tril[boundary])
    sel = np.concatenate(blocks, axis=0)
    return jnp.asarray(np.concatenate([sel, sel, sel], axis=1), dtype=BF16)


def _gla_kernel(q_ref, k_ref, v_ref, z_ref, gd_ref, wg_ref, bg_ref, ng_ref, s0_ref, gsel_ref,
                y_ref, s_ref, st_ref, *, C):
    li = pl.program_id(2)
    HG = GLA_HEADS_PER_STEP

    @pl.when(li == 0)
    def _():
        for h in range(HG):
            st_ref[h] = s0_ref[0, h].T

    row = _iota2((C, C), 0)
    col = _iota2((C, C), 1)
    eye = row == col
    masks = [((row // (2 * s)) == (col // (2 * s))) & ((row % (2 * s)) >= s) & ((col % (2 * s)) < s)
             for s in _gla_levels(C)]
    gsel = gsel_ref[...]
    TL = q_ref.shape[1]
    probs = [(h, slice(c * C, (c + 1) * C)) for h in range(HG) for c in range(TL // C)]
    kcols = lambda h: slice(h * GLA_HK, (h + 1) * GLA_HK)
    vcols = lambda h: slice(h * GLA_HV, (h + 1) * GLA_HV)

    def stage_decay(prob):
        h, sl = prob
        x = _dot(gd_ref[0, sl, :].astype(BF16), wg_ref[:, kcols(h)]) + bg_ref[:, kcols(h)]
        la = (jnp.minimum(x, 0.0) - jnp.log1p(jnp.exp(-jnp.abs(x)))) * (1.0 / GLA_TAU)
        return dict(h=h, sl=sl, d_all=_cumsum_rows(gsel, la))

    def stage_scores(p):
        h, sl = p['h'], p['sl']
        q = q_ref[0, sl, kcols(h)].astype(F32) * (GLA_HK ** -0.5)
        k = k_ref[0, sl, kcols(h)].astype(F32)
        d_all = p['d_all']
        b = d_all[:C]
        bend = b[C - 1:C, :]
        a = jnp.where(eye, jnp.sum(q * k, axis=-1, keepdims=True), 0.0)
        for lvl, mask in enumerate(masks):
            d = b - d_all[(lvl + 1) * C:(lvl + 2) * C]
            qs = q * jnp.exp(jnp.minimum(d, 0.0))
            ks = k * jnp.exp(jnp.minimum(-d, 0.0))
            a = jnp.where(mask, _dot_nt(qs.astype(BF16), ks.astype(BF16)), a)
        return dict(h=h, sl=sl, a=a.astype(BF16), qd=(q * jnp.exp(b)).astype(BF16),
                    kd=(k * jnp.exp(bend - b)).astype(BF16), wend=jnp.exp(bend))

    def stage_values(p):
        v = v_ref[0, p['sl'], vcols(p['h'])].astype(BF16)
        p.update(o=_dot(p['a'], v), upd=_dot_tn(v, p['kd']))
        return p

    ps = [stage_decay(prob) for prob in probs]
    ps = [stage_scores(p) for p in ps]
    ps = [stage_values(p) for p in ps]

    sts = [st_ref[h] for h in range(HG)]
    for p in ps:
        h, sl = p['h'], p['sl']
        o = p['o'] + _dot_nt(p['qd'], sts[h].astype(BF16))
        sts[h] = sts[h] * p['wend'] + p['upd']
        o = o * lax.rsqrt(jnp.mean(o * o, axis=-1, keepdims=True) + NORM_EPS) * ng_ref[:, vcols(h)]
        y_ref[0, sl, vcols(h)] = (o * _silu(z_ref[0, sl, vcols(h)].astype(F32))).astype(BF16)
    for h in range(HG):
        st_ref[h] = sts[h]

    @pl.when(li == pl.num_programs(2) - 1)
    def _():
        for h in range(HG):
            s_ref[0, h] = sts[h].T


def _gla(proj, wg_pad, bg, ng, s0, C):
    B, L, _ = proj.shape
    TL = min(L, LONG_TL)
    HG = GLA_HEADS_PER_STEP
    KW, VW = HG * GLA_HK, HG * GLA_HV
    qb, kb = COL_Q // KW, COL_K // KW
    vb, zb = COL_V // VW, COL_ZGLA // VW
    gb = COL_GD // LANE
    gsel = _gla_select_matrix(C)
    st_spec = pl.BlockSpec((1, HG, GLA_HK, GLA_HV), lambda b, h, l: (b, h, 0, 0))
    return pl.pallas_call(
        functools.partial(_gla_kernel, C=C),
        grid=(B, GLA_HEADS // HG, L // TL),
        in_specs=[
            pl.BlockSpec((1, TL, KW), lambda b, h, l: (b, l, qb + h)),
            pl.BlockSpec((1, TL, KW), lambda b, h, l: (b, l, kb + h)),
            pl.BlockSpec((1, TL, VW), lambda b, h, l: (b, l, vb + h)),
            pl.BlockSpec((1, TL, VW), lambda b, h, l: (b, l, zb + h)),
            pl.BlockSpec((1, TL, LANE), lambda b, h, l: (b, l, gb)),
            pl.BlockSpec((LANE, KW), lambda b, h, l: (0, h)),
            pl.BlockSpec((1, KW), lambda b, h, l: (0, h)),
            pl.BlockSpec((1, VW), lambda b, h, l: (0, h)),
            st_spec,
            pl.BlockSpec(gsel.shape, lambda b, h, l: (0, 0)),
        ],
        out_specs=[
            pl.BlockSpec((1, TL, VW), lambda b, h, l: (b, l, h)),
            st_spec,
        ],
        out_shape=[
            jax.ShapeDtypeStruct((B, L, GLA_DV), BF16),
            jax.ShapeDtypeStruct((B, GLA_HEADS, GLA_HK, GLA_HV), F32),
        ],
        scratch_shapes=[pltpu.VMEM((HG, GLA_HV, GLA_HK), F32)],
        compiler_params=_cparams(("parallel", "parallel", "arbitrary")),
        name="gla",
    )(proj, proj, proj, proj, proj, wg_pad, bg, ng, s0, gsel)


def _lru_kernel(x_ref, z_ref, c0_ref, h0_ref, cw_ref, cb_ref, wax_ref, ba_ref, bx_ref, lam_ref,
                y_ref, hl_ref, xpad_ref, a_ref, u_ref, hs_ref, h_ref):
    li = pl.program_id(1)
    TL = x_ref.shape[1]
    PAD = SUBLANE

    @pl.when(li == 0)
    def _():
        xpad_ref[0:PAD, :] = c0_ref[0]
        h_ref[...] = h0_ref[0]

    xpad_ref[PAD:PAD + TL, :] = x_ref[0].astype(F32)
    xp = xpad_ref[...]
    acc = xp * cw_ref[0:1, :]
    for t in range(1, CONV_WIDTH):
        acc = pltpu.roll(acc, 1, axis=0) + xp * cw_ref[t:t + 1, :]
    xc = acc[PAD:, :] + cb_ref[...]
    xc_b = xc.astype(BF16)
    rs, xs = [], []
    for n in range(LRU_BLOCKS):
        g = _dot(xc_b[:, n * LRU_BS:(n + 1) * LRU_BS], wax_ref[n])
        rs.append(g[:, :LRU_BS])
        xs.append(g[:, LRU_BS:])
    r = _sigmoid(jnp.concatenate(rs, axis=1) + ba_ref[...])
    i = _sigmoid(jnp.concatenate(xs, axis=1) + bx_ref[...])
    log_a = r * ((-LRU_C) * _softplus(-lam_ref[...]))
    a_ref[...] = jnp.exp(log_a)
    th = jnp.tanh(log_a)
    u_ref[...] = jnp.sqrt(-2.0 * th / (1.0 - th)) * (i * xc)

    row8 = _iota2((SUBLANE, LRU_WIDTH), 0)

    def body(g, h):
        base = pl.multiple_of(g * SUBLANE, SUBLANE)
        a = a_ref[pl.ds(base, SUBLANE), :]
        u = u_ref[pl.ds(base, SUBLANE), :]
        s = 1
        while s < SUBLANE:
            keep = row8 >= s
            u = u + a * jnp.where(keep, pltpu.roll(u, s, axis=0), 0.0)
            a = a * jnp.where(keep, pltpu.roll(a, s, axis=0), 1.0)
            s *= 2
        hs = a * h + u
        hs_ref[pl.ds(base, SUBLANE), :] = hs
        return hs[SUBLANE - 1:SUBLANE, :]

    h = lax.fori_loop(0, TL // SUBLANE, body, h_ref[...], unroll=2)
    h_ref[...] = h
    hl_ref[0] = h
    y_ref[0] = (hs_ref[...] * _silu(z_ref[0].astype(F32))).astype(BF16)
    xpad_ref[0:PAD, :] = xpad_ref[TL:TL + PAD, :]


def _lru(proj, conv0_pad, h0, cw, cb, wax, ba, bx, lam):
    B, L, _ = proj.shape
    TL = min(L, LONG_TL)
    W = LRU_WIDTH
    xb, zb = COL_XL // W, COL_ZLRU // W
    vec = pl.BlockSpec((1, W), lambda b, l: (0, 0))
    return pl.pallas_call(
        _lru_kernel,
        grid=(B, L // TL),
        in_specs=[
            pl.BlockSpec((1, TL, W), lambda b, l: (b, l, xb)),
            pl.BlockSpec((1, TL, W), lambda b, l: (b, l, zb)),
            pl.BlockSpec((1, SUBLANE, W), lambda b, l: (b, 0, 0)),
            pl.BlockSpec((1, 1, W), lambda b, l: (b, 0, 0)),
            pl.BlockSpec((CONV_WIDTH, W), lambda b, l: (0, 0)),
            vec,
            pl.BlockSpec((LRU_BLOCKS, LRU_BS, 2 * LRU_BS), lambda b, l: (0, 0, 0)),
            vec, vec, vec,
        ],
        out_specs=[
            pl.BlockSpec((1, TL, W), lambda b, l: (b, l, 0)),
            pl.BlockSpec((1, 1, W), lambda b, l: (b, 0, 0)),
        ],
        out_shape=[
            jax.ShapeDtypeStruct((B, L, W), BF16),
            jax.ShapeDtypeStruct((B, 1, W), F32),
        ],
        scratch_shapes=[
            pltpu.VMEM((TL + SUBLANE, W), F32),
            pltpu.VMEM((TL, W), F32),
            pltpu.VMEM((TL, W), F32),
            pltpu.VMEM((TL, W), F32),
            pltpu.VMEM((1, W), F32),
        ],
        compiler_params=_cparams(("parallel", "arbitrary")),
        name="rglru",
    )(proj, proj, conv0_pad, h0, cw, cb, wax, ba, bx, lam)


def _headsum(x, ones_bd):
    xb = x.astype(BF16)
    outs = [_dot(xb[:, g * RWKV_GW:(g + 1) * RWKV_GW], ones_bd) for g in range(RWKV_NGROUPS)]
    return jnp.concatenate(outs, axis=1)


def _rwkv_local(loads, C, masks):
    tril_b, strict_rb, incl_rb, eye_rb, bd_sq, bd_cv, bd_state = masks

    def tile_rows(x, n):
        return jnp.concatenate([x] * n, axis=0)

    def blockdiag_cv(x):
        return tile_rows(x.astype(BF16), RWKV_GROUP) * bd_cv

    def blockdiag_sq(x):
        return tile_rows(x.astype(BF16), RWKV_GROUP) * bd_sq

    def stage_decay(load):
        r, k, v, kk, a, lw = load()
        lc = _cumsum_rows(tril_b, lw)
        wend = jnp.exp(lc[C - 1:C, :])
        einv = jnp.exp(-lc)
        bt = kk * jnp.exp(lc - lw)
        rt = r * jnp.exp(lc)
        at = -(kk * a) * einv
        kt = k * einv
        return dict(v=v, wend=wend, bt=bt, rt=rt, at=at, kt=kt)

    def stage_scores(p):
        br = jnp.concatenate([p['bt'], p['rt']], axis=0).astype(BF16)
        pa = _dot_nt(br, blockdiag_cv(p['at']))
        pk = _dot_nt(br, blockdiag_cv(p['kt']))
        a_ba = jnp.where(strict_rb, pa[:C], 0.0)
        p.update(a_ra=jnp.where(incl_rb, pa[C:], 0.0).astype(BF16),
                 a_bk=jnp.where(strict_rb, pk[:C], 0.0).astype(BF16),
                 a_rk=jnp.where(incl_rb, pk[C:], 0.0).astype(BF16),
                 x=a_ba, xb=blockdiag_sq(a_ba), t=eye_rb + a_ba)
        return p

    def stage_values(p):
        xv = _dot(jnp.concatenate([p['a_bk'], p['a_rk']], axis=0), blockdiag_cv(p['v']))
        p.update(xv=xv[:C], yv=xv[C:])
        return p

    def stage_square(p):
        p['x'] = _dot(p['x'].astype(BF16), p['xb'])
        p['xb'] = blockdiag_sq(p['x'])
        return p

    def stage_double(p, last):
        if last:
            p['t'] = p['t'] + _dot(p['t'].astype(BF16), p['xb'])
            return p
        xt = _dot(jnp.concatenate([p['x'], p['t']], axis=0).astype(BF16), p['xb'])
        p['x'] = xt[:C]
        p['t'] = p['t'] + xt[C:]
        p['xb'] = blockdiag_sq(p['x'])
        return p

    def stage_apply(p):
        tb = p['t'].astype(BF16)
        p.update(bhat=_dot(tb, blockdiag_cv(p['bt'])), uloc=_dot(tb, blockdiag_cv(p['xv'])))
        return p

    def stage_out(p):
        rhat = p['rt'] + _dot(p['a_ra'], blockdiag_cv(p['bhat']))
        yloc = _dot(p['a_ra'], blockdiag_cv(p['uloc'])) + p['yv']
        atw = (p['at'] * p['wend']).astype(BF16)
        ktw = (p['kt'] * p['wend']).astype(BF16)
        ghat = jnp.where(bd_state, _dot_tn(p['bhat'].astype(BF16), atw), 0.0)
        h = jnp.where(bd_state,
                      _dot_tn(jnp.concatenate([p['uloc'], p['v']], axis=0).astype(BF16),
                              jnp.concatenate([atw, ktw], axis=0)), 0.0)
        return rhat, yloc, ghat, h, p['wend']

    ps = [stage_decay(ld) for ld in loads]
    ps = [stage_scores(p) for p in ps]
    ps = [stage_values(p) for p in ps]
    ps = [stage_square(p) for p in ps]
    n = 4
    while n <= C:
        ps = [stage_double(p, last=(n == C)) for p in ps]
        n *= 2
    ps = [stage_apply(p) for p in ps]
    return [stage_out(p) for p in ps]


def _rwkv_masks(C):
    G = RWKV_GROUP
    tril = (_iota2((C, 3 * C), 0) >= (_iota2((C, 3 * C), 1) % C)).astype(BF16)
    r_rb = _iota2((C, G * C), 0)
    c_rb = _iota2((C, G * C), 1) % C
    strict_rb = r_rb > c_rb
    incl_rb = r_rb >= c_rb
    eye_rb = (r_rb == c_rb).astype(F32)
    bd_sq = ((_iota2((G * C, G * C), 0) // C) == (_iota2((G * C, G * C), 1) // C)).astype(BF16)
    bd_cv = ((_iota2((G * C, RWKV_GW), 0) // C) == (_iota2((G * C, RWKV_GW), 1) // RWKV_HD)).astype(BF16)
    bd_state = (_iota2((RWKV_GW, RWKV_GW), 0) // RWKV_HD) == (_iota2((RWKV_GW, RWKV_GW), 1) // RWKV_HD)
    return tril, strict_rb, incl_rb, eye_rb, bd_sq, bd_cv, bd_state


def _rwkv_kernel(r_ref, k_ref, v_ref, wa_ref, z_ref, sh_rkv_ref, sh_wa_ref, s0_ref,
                 mu_rkv_ref, mu_wa_ref, w0_ref, wup_ref, a0_ref, aup_ref, kk_ref, ka_ref, rk_ref,
                 gnw_ref, gnb_ref,
                 y_ref, s_ref,
                 xpad_ref, wapad_ref, st_ref, rs_ref, ks_ref, vs_ref, kks_ref, as_ref, lws_ref, ys_ref,
                 rhat_ref, ghat_ref, h_ref, wend_ref, *, C):
    li = pl.program_id(1)
    TL = r_ref.shape[1]
    PAD = SUBLANE
    W = RWKV_WIDTH

    @pl.when(li == 0)
    def _():
        for n in range(3):
            xpad_ref[n, PAD - 1:PAD, :] = sh_rkv_ref[0, :, n * W:(n + 1) * W]
        wapad_ref[PAD - 1:PAD, :] = sh_wa_ref[0]
        zero = jnp.zeros((RWKV_HD, RWKV_HD), F32)
        for g in range(RWKV_NGROUPS):
            st_ref[g] = jnp.concatenate(
                [jnp.concatenate([s0_ref[0, g * RWKV_GROUP + i] if i == j else zero
                                  for i in range(RWKV_GROUP)], axis=1)
                 for j in range(RWKV_GROUP)], axis=0)

    mixed = []
    for n, ref in enumerate((r_ref, k_ref, v_ref)):
        x = ref[0].astype(F32)
        xpad_ref[n, PAD:PAD + TL, :] = x
        prev = xpad_ref[n, PAD - 1:PAD - 1 + TL, :]
        mixed.append(x + (prev - x) * mu_rkv_ref[:, n * W:(n + 1) * W])
        xpad_ref[n, PAD - 1:PAD, :] = xpad_ref[n, PAD + TL - 1:PAD + TL, :]
    r, k, v = mixed
    xwa = wa_ref[0].astype(F32)
    wapad_ref[PAD:PAD + TL, :] = xwa
    prev = wapad_ref[PAD - 1:PAD - 1 + TL, :]
    wapad_ref[PAD - 1:PAD, :] = wapad_ref[PAD + TL - 1:PAD + TL, :]
    xwa = xwa + (prev - xwa) * mu_wa_ref[...]

    w = w0_ref[...] + _dot(jnp.tanh(xwa).astype(BF16), wup_ref[...])
    lw = (-RWKV_DECAY_SCALE) * _sigmoid(w)
    a = _sigmoid(a0_ref[...] + _dot(xwa.astype(BF16), aup_ref[...]))

    ones_bd = ((_iota2((RWKV_GW, RWKV_GW), 0) // RWKV_HD)
               == (_iota2((RWKV_GW, RWKV_GW), 1) // RWKV_HD)).astype(BF16)
    kk = k * kk_ref[...]
    kk = kk * lax.rsqrt(_headsum(kk * kk, ones_bd) + 1e-12)
    k = k * (1.0 + (a - 1.0) * ka_ref[...])
    rs_ref[...] = r
    ks_ref[...] = k
    vs_ref[...] = v
    kks_ref[...] = kk
    as_ref[...] = a
    lws_ref[...] = lw

    masks = _rwkv_masks(C)

    NCH = TL // C
    NB = min(NCH, RWKV_LOCAL_CHUNKS)

    def loader(rows, cols):
        return lambda: tuple(ref[rows, cols]
                             for ref in (rs_ref, ks_ref, vs_ref, kks_ref, as_ref, lws_ref))

    for cb in range(NCH // NB):
        probs = []
        for c in range(cb * NB, (cb + 1) * NB):
            rows = slice(c * C, (c + 1) * C)
            for g in range(RWKV_NGROUPS):
                probs.append((c, rows, g, slice(g * RWKV_GW, (g + 1) * RWKV_GW)))
        outs = _rwkv_local([loader(rows, cols) for _, rows, _, cols in probs], C, masks)
        for (c, rows, g, cols), (rhat, yloc, ghat, h, wend) in zip(probs, outs):
            rhat_ref[rows, cols] = rhat.astype(BF16)
            ys_ref[rows, cols] = yloc
            ghat_ref[c, g] = ghat.astype(BF16)
            h_ref[c, g] = h
            wend_ref[c, :, cols] = wend

    states = [st_ref[g] for g in range(RWKV_NGROUPS)]
    for c in range(NCH):
        rows = slice(c * C, (c + 1) * C)
        for g in range(RWKV_NGROUPS):
            cols = slice(g * RWKV_GW, (g + 1) * RWKV_GW)
            sb = states[g].astype(BF16)
            ys_ref[rows, cols] = ys_ref[rows, cols] + _dot_nt(rhat_ref[rows, cols], sb)
            states[g] = states[g] * wend_ref[c, :, cols] + _dot(sb, ghat_ref[c, g]) + h_ref[c, g]
    for g in range(RWKV_NGROUPS):
        st_ref[g] = states[g]

    y = ys_ref[...]
    r = rs_ref[...]
    k = ks_ref[...]
    v = vs_ref[...]
    inv_hd = 1.0 / RWKV_HD
    mean = _headsum(y, ones_bd) * inv_hd
    d = y - mean
    var = _headsum(d * d, ones_bd) * inv_hd
    yn = d * lax.rsqrt(var + RWKV_GN_EPS) * gnw_ref[...] + gnb_ref[...]
    yn = yn + _headsum(r * k * rk_ref[...], ones_bd) * v
    y_ref[0] = (yn * _silu(z_ref[0].astype(F32))).astype(BF16)

    @pl.when(li == pl.num_programs(1) - 1)
    def _():
        for g in range(RWKV_NGROUPS):
            for j in range(RWKV_GROUP):
                d = slice(j * RWKV_HD, (j + 1) * RWKV_HD)
                s_ref[0, g * RWKV_GROUP + j] = states[g][d, d]


def _rwkv(proj, sh_rkv, sh_wa, s0, mu_rkv, mu_wa, w0, wup_pad, a0, aup_pad, k_k, k_a, r_k, gn_w, gn_b, C):
    B, L, _ = proj.shape
    TL = min(L, MIXER_TL)
    W = RWKV_WIDTH
    rb, kb, vb, zb = COL_R // W, COL_KRW // W, COL_VRW // W, COL_ZRW // W
    wab = COL_WA // LANE
    vec = pl.BlockSpec((1, W), lambda b, l: (0, 0))
    mat = pl.BlockSpec((LANE, W), lambda b, l: (0, 0))
    tok = lambda cb: pl.BlockSpec((1, TL, W), lambda b, l: (b, l, cb))
    st_spec = pl.BlockSpec((1, RWKV_HEADS, RWKV_HD, RWKV_HD), lambda b, l: (b, 0, 0, 0))
    return pl.pallas_call(
        functools.partial(_rwkv_kernel, C=C),
        grid=(B, L // TL),
        in_specs=[
            tok(rb), tok(kb), tok(vb),
            pl.BlockSpec((1, TL, LANE), lambda b, l: (b, l, wab)),
            tok(zb),
            pl.BlockSpec((1, 1, 3 * W), lambda b, l: (b, 0, 0)),
            pl.BlockSpec((1, 1, LANE), lambda b, l: (b, 0, 0)),
            st_spec,
            pl.BlockSpec((1, 3 * W), lambda b, l: (0, 0)),
            pl.BlockSpec((1, LANE), lambda b, l: (0, 0)),
            vec, mat, vec, mat, vec, vec, vec, vec, vec,
        ],
        out_specs=[
            pl.BlockSpec((1, TL, W), lambda b, l: (b, l, 0)),
            st_spec,
        ],
        out_shape=[
            jax.ShapeDtypeStruct((B, L, W), BF16),
            jax.ShapeDtypeStruct((B, RWKV_HEADS, RWKV_HD, RWKV_HD), F32),
        ],
        scratch_shapes=[
            pltpu.VMEM((3, TL + SUBLANE, W), F32),
            pltpu.VMEM((TL + SUBLANE, LANE), F32),
            pltpu.VMEM((RWKV_NGROUPS, RWKV_GW, RWKV_GW), F32),
        ] + [pltpu.VMEM((TL, W), F32)] * 7 + [
            pltpu.VMEM((TL, W), BF16),
            pltpu.VMEM((TL // C, RWKV_NGROUPS, RWKV_GW, RWKV_GW), BF16),
            pltpu.VMEM((TL // C, RWKV_NGROUPS, RWKV_GW, RWKV_GW), F32),
            pltpu.VMEM((TL // C, 1, W), F32),
        ],
        compiler_params=_cparams(("parallel", "arbitrary")),
        name="rwkv7",
    )(proj, proj, proj, proj, proj, sh_rkv, sh_wa, s0,
      mu_rkv, mu_wa, w0, wup_pad, a0, aup_pad, k_k, k_a, r_k, gn_w, gn_b)


def _outproj_kernel(x_ref, yg_ref, yl_ref, yr_ref, ga_ref, gb_ref, gc_ref,
                    wg_ref, wl_ref, wr_ref, wo_ref, fg_ref, o_ref, *, final):
    merged = (_sigmoid(ga_ref[...].astype(F32)) * _dot(yg_ref[...], wg_ref[...])
              + _sigmoid(gb_ref[...].astype(F32)) * _dot(yl_ref[...], wl_ref[...])
              + _sigmoid(gc_ref[...].astype(F32)) * _dot(yr_ref[...], wr_ref[...]))
    x = x_ref[...] + _dot(merged.astype(BF16), wo_ref[...])
    if final:
        x = x * lax.rsqrt(jnp.mean(x * x, axis=-1, keepdims=True) + NORM_EPS) * fg_ref[...]
    o_ref[...] = x


def _outproj(x2d, yg, yl, yr, proj2d, wg, wl, wr, wo, fg, final):
    T = x2d.shape[0]
    tm = min(T, OUTPROJ_TM)
    D = D_MODEL
    gb = COL_GATES // D
    tok = lambda cb: pl.BlockSpec((tm, D), lambda i: (i, cb))
    wspec = pl.BlockSpec((D, D), lambda i: (0, 0))
    return pl.pallas_call(
        functools.partial(_outproj_kernel, final=final),
        grid=(T // tm,),
        in_specs=[tok(0), tok(0), tok(0), tok(0), tok(gb), tok(gb + 1), tok(gb + 2),
                  wspec, wspec, wspec, wspec, pl.BlockSpec((1, D), lambda i: (0, 0))],
        out_specs=tok(0),
        out_shape=jax.ShapeDtypeStruct((T, D), F32),
        compiler_params=_cparams(("parallel",)),
        name="outproj",
    )(x2d, yg, yl, yr, proj2d, proj2d, proj2d, wg, wl, wr, wo, fg)


def _relayout_kernel(w_ref, o_ref):
    o_gd = GLA_DK + GLA_DK + GLA_DV
    o_zg = o_gd + GLA_RANK
    o_wa = o_zg + GLA_DV + 2 * LRU_WIDTH + 3 * RWKV_WIDTH
    o_zr = o_wa + DECAY_RANK + AAA_RANK
    d_in = w_ref.shape[1]
    pieces = [(COL_Q, 0, o_gd),
              (COL_ZGLA, o_zg, o_wa),
              (COL_ZRW, o_zr, d_in),
              (COL_GD, o_gd, o_zg),
              (COL_WA, o_wa, o_zr)]
    for dst, lo, hi in pieces:
        o_ref[0, dst:dst + hi - lo, :] = w_ref[0, lo:hi, :].astype(BF16)
    o_ref[0, COL_GD + GLA_RANK:COL_WA, :] = jnp.zeros((COL_WA - COL_GD - GLA_RANK, o_ref.shape[2]), BF16)


def _relayout_w_in(w_in):
    wt = jnp.swapaxes(w_in, 1, 2)
    d_in = wt.shape[1]
    return pl.pallas_call(
        _relayout_kernel,
        grid=(DEPTH, D_MODEL // LANE),
        in_specs=[pl.BlockSpec((1, d_in, LANE), lambda l, i: (l, 0, i))],
        out_specs=pl.BlockSpec((1, D_PROJ, LANE), lambda l, i: (l, 0, i)),
        out_shape=jax.ShapeDtypeStruct((DEPTH, D_PROJ, D_MODEL), BF16),
        compiler_params=_cparams(("parallel", "parallel")),
        name="relayout_w_in",
    )(wt)


def _prep_layer(P, l):
    zpad = jnp.zeros((LANE - GLA_RANK, GLA_DK), F32)
    z64 = jnp.zeros((DECAY_RANK, RWKV_WIDTH), F32)
    mu = P['rwkv_mu'][l]
    row = lambda a: a.reshape(1, -1)
    return dict(
        norm_g=row(P['norm_g'][l]),
        wg_pad=jnp.concatenate([P['gla_w_gup'][l], zpad], axis=0).astype(BF16),
        bg=row(P['gla_b_g'][l]), ng=row(P['gla_norm_g'][l]),
        cw=P['lru_conv_w'][l], cb=row(P['lru_conv_b'][l]),
        wax=jnp.concatenate([P['lru_w_a'][l], P['lru_w_x'][l]], axis=-1).astype(BF16),
        ba=row(P['lru_b_a'][l]), bx=row(P['lru_b_x'][l]), lam=row(P['lru_lambda'][l]),
        mu_rkv=row(mu[:3 * RWKV_WIDTH]), mu_wa=row(mu[3 * RWKV_WIDTH:]),
        w0=row(P['rwkv_w0'][l]),
        wup_pad=jnp.concatenate([P['rwkv_w_up'][l], z64], axis=0).astype(BF16),
        a0=row(P['rwkv_a0'][l]),
        aup_pad=jnp.concatenate([z64, P['rwkv_a_up'][l]], axis=0).astype(BF16),
        k_k=row(P['rwkv_k_k'][l]), k_a=row(P['rwkv_k_a'][l]), r_k=row(P['rwkv_r_k'][l]),
        gn_w=row(P['rwkv_gn_w'][l]), gn_b=row(P['rwkv_gn_b'][l]),
        wpg=P['w_proj_gla'][l].astype(BF16), wpl=P['w_proj_lru'][l].astype(BF16),
        wpr=P['w_proj_rwkv'][l].astype(BF16), wo=P['w_out'][l].astype(BF16),
    )


def _to_kernel_states(gla, lru_h, lru_conv, rwkv, shift):
    Bn = gla.shape[1]
    conv_pad = jnp.zeros((DEPTH, Bn, SUBLANE - (CONV_WIDTH - 1), LRU_WIDTH), lru_conv.dtype)
    return (gla, lru_h[:, :, None, :], jnp.concatenate([conv_pad, lru_conv], axis=2), rwkv,
            shift[:, :, None, :3 * RWKV_WIDTH], shift[:, :, None, 3 * RWKV_WIDTH:])


def _zero_kernel_states(Bn, dt):
    z = lambda *shape: jnp.zeros((DEPTH, Bn) + shape, dt)
    return (z(GLA_HEADS, GLA_HK, GLA_HV), z(1, LRU_WIDTH), z(SUBLANE, LRU_WIDTH),
            z(RWKV_HEADS, RWKV_HD, RWKV_HD), z(1, 3 * RWKV_WIDTH), z(1, LANE))


def _run_trunk(x, states, layers, w_all, final_g, C):
    gla0, lru_h0, conv0_pad, rwkv0, sh_rkv, sh_wa = states
    B, L, D = x.shape
    T = B * L
    x2d = x.reshape(T, D)
    n_gla, n_h, n_conv, n_rw, n_shift = [], [], [], [], []
    for l in range(DEPTH):
        p = layers[l]
        tiles_per_seq = L // INPROJ_TM if L % INPROJ_TM == 0 else 0
        proj2d, tails = _inproj(x2d, p['norm_g'], w_all, l, BF16 if tiles_per_seq else F32)
        proj = proj2d.reshape(B, L, D_PROJ)
        if tiles_per_seq:
            tail = tails.reshape(B, tiles_per_seq, SUBLANE, D_PROJ)[:, -1]
        else:
            tail = proj[:, L - SUBLANE:, :]

        yg, s_gla = _gla(proj, p['wg_pad'], p['bg'], p['ng'], gla0[l], C)
        yl, h_last = _lru(proj, conv0_pad[l], lru_h0[l], p['cw'], p['cb'], p['wax'],
                          p['ba'], p['bx'], p['lam'])
        yr, s_rw = _rwkv(proj, sh_rkv[l], sh_wa[l], rwkv0[l],
                         p['mu_rkv'], p['mu_wa'], p['w0'], p['wup_pad'], p['a0'], p['aup_pad'],
                         p['k_k'], p['k_a'], p['r_k'], p['gn_w'], p['gn_b'], C)

        x2d = _outproj(x2d, yg.reshape(T, D), yl.reshape(T, D), yr.reshape(T, D), proj2d,
                       p['wpg'], p['wpl'], p['wpr'], p['wo'], final_g, final=(l == DEPTH - 1))

        n_gla.append(s_gla)
        n_h.append(h_last)
        n_conv.append(tail[:, SUBLANE - (CONV_WIDTH - 1):, COL_XL:COL_XL + LRU_WIDTH])
        n_rw.append(s_rw)
        n_shift.append(jnp.concatenate(
            [tail[:, -1, COL_R:COL_R + 3 * RWKV_WIDTH], tail[:, -1, COL_WA:COL_WA + 2 * DECAY_RANK]],
            axis=-1))
    return (x2d.reshape(B, L, D), jnp.stack(n_gla), jnp.stack(n_h)[:, :, 0, :],
            jnp.stack(n_conv), jnp.stack(n_rw), jnp.stack(n_shift))


def kernel(x_prompt, x_sample, state_gla, state_lru_h, state_lru_conv, state_rwkv, state_rwkv_shift,
           norm_g, w_in, gla_w_gup, gla_b_g, gla_norm_g,
           lru_conv_w, lru_conv_b, lru_w_a, lru_b_a, lru_w_x, lru_b_x, lru_lambda,
           rwkv_mu, rwkv_w0, rwkv_w_up, rwkv_a0, rwkv_a_up, rwkv_k_k, rwkv_k_a, rwkv_r_k,
           rwkv_gn_w, rwkv_gn_b, w_proj_gla, w_proj_lru, w_proj_rwkv, w_out, final_norm_g):
    P = dict(norm_g=norm_g, w_in=w_in, gla_w_gup=gla_w_gup, gla_b_g=gla_b_g, gla_norm_g=gla_norm_g,
             lru_conv_w=lru_conv_w, lru_conv_b=lru_conv_b, lru_w_a=lru_w_a, lru_b_a=lru_b_a,
             lru_w_x=lru_w_x, lru_b_x=lru_b_x, lru_lambda=lru_lambda,
             rwkv_mu=rwkv_mu, rwkv_w0=rwkv_w0, rwkv_w_up=rwkv_w_up, rwkv_a0=rwkv_a0,
             rwkv_a_up=rwkv_a_up, rwkv_k_k=rwkv_k_k, rwkv_k_a=rwkv_k_a, rwkv_r_k=rwkv_r_k,
             rwkv_gn_w=rwkv_gn_w, rwkv_gn_b=rwkv_gn_b, w_proj_gla=w_proj_gla, w_proj_lru=w_proj_lru,
             w_proj_rwkv=w_proj_rwkv, w_out=w_out)
    layers = [_prep_layer(P, l) for l in range(DEPTH)]
    w_all = _relayout_w_in(w_in)
    final_g = final_norm_g.reshape(1, -1)
    out_p = _run_trunk(x_prompt, _zero_kernel_states(x_prompt.shape[0], x_prompt.dtype),
                       layers, w_all, final_g, C=CHUNK)
    Ls = x_sample.shape[1]
    out_s = _run_trunk(x_sample, _to_kernel_states(state_gla, state_lru_h, state_lru_conv,
                                                   state_rwkv, state_rwkv_shift),
                       layers, w_all, final_g, C=CHUNK if Ls % CHUNK == 0 else Ls)
    return (out_p[0], out_s[0]) + tuple(out_p[1:]) + tuple(out_s[1:])
```
